```python
import math
import jax, jax.numpy as jnp
from jax import lax
import numpy as np

D_MODEL = 4096
BATCH = 2
SEQ = 4096
DEPTH = 2

N_EVEN = (DEPTH + 1) // 2
N_ODD = DEPTH // 2
NORM_EPS = 1e-6
NEG = -1e30
CONV_WIDTH = 4

MOBA_HEAD_DIM = 128
MOBA_HEADS = D_MODEL // 256
MOBA_WIDTH = MOBA_HEADS * MOBA_HEAD_DIM
MOBA_BLOCK = 256
MOBA_TOPK = 3
MOBA_QCHUNK = 32

LRU_WIDTH = D_MODEL // 2
LRU_BLOCKS = 16
LRU_BLOCK_DIM = LRU_WIDTH // LRU_BLOCKS
LRU_C = 8.0

L0_IN = 3 * MOBA_WIDTH + 2 * LRU_WIDTH
L0_MIX = MOBA_WIDTH + LRU_WIDTH

GLA_HEADS = 4
GLA_DV = D_MODEL // 8
GLA_DK = GLA_DV // 2
GLA_KEY = GLA_HEADS * GLA_DK
GLA_VAL = GLA_HEADS * GLA_DV
GLA_RANK = 16
GLA_TAU = 16.0
GLA_CHUNK = 16

SSD_WIDTH = D_MODEL // 2
SSD_HEAD_DIM = 64
SSD_HEADS = SSD_WIDTH // SSD_HEAD_DIM
SSD_GROUPS = 4
SSD_STATE = 128
SSD_CHUNK = 128
SSD_CONV_DIM = SSD_WIDTH + 2 * SSD_GROUPS * SSD_STATE

L1_IN = 2 * GLA_KEY + 2 * GLA_VAL + GLA_RANK + SSD_WIDTH + SSD_CONV_DIM + SSD_HEADS
L1_MIX = GLA_VAL + SSD_WIDTH

FFN_DENSE = 14336
N_EXPERTS = 8
TOP_K = 2
FFN_EXPERT = 4096

kernel_name = 'hybrid_moba_rglru_gla_ssd_moe'


def rms_norm(x, g, eps=NORM_EPS):
    xf = x.astype(jnp.float32)
    y = xf * lax.rsqrt(jnp.mean(xf * xf, axis=-1, keepdims=True) + eps)
    return (y * g.astype(jnp.float32)).astype(x.dtype)


def split_cols(t, sizes):
    out, start = [], 0
    for s in sizes:
        out.append(t[..., start:start + s])
        start += s
    return out


def causal_dwconv(x, w, b):
    c = x.shape[-1]
    y = lax.conv_general_dilated(x, w[:, None, :].astype(x.dtype), window_strides=(1,),
                                 padding=[(CONV_WIDTH - 1, 0)],
                                 dimension_numbers=('NWC', 'WIO', 'NWC'),
                                 feature_group_count=c)
    return y + b.astype(x.dtype)


def swiglu(h, w1, w3, w2):
    return (jax.nn.silu(h @ w1) * (h @ w3)) @ w2


def moba_attention(q, k, v):
    bsz, t, nh, dh = q.shape
    nb = -(-t // MOBA_BLOCK)
    pad = nb * MOBA_BLOCK - t
    q = jnp.transpose(q, (0, 2, 1, 3)) * (dh ** -0.5)
    k = jnp.pad(jnp.transpose(k, (0, 2, 1, 3)), ((0, 0), (0, 0), (0, pad), (0, 0)))
    v = jnp.pad(jnp.transpose(v, (0, 2, 1, 3)), ((0, 0), (0, 0), (0, pad), (0, 0)))
    k_blocks = k.reshape(bsz, nh, nb, MOBA_BLOCK, dh)
    v_blocks = v.reshape(bsz, nh, nb, MOBA_BLOCK, dh)
    k_mean = jnp.mean(k_blocks.astype(jnp.float32), axis=3)
    gate = jnp.einsum('bhtd,bhnd->bhtn', q.astype(jnp.float32), k_mean)
    q_blk = jnp.arange(t) // MOBA_BLOCK
    past = jnp.arange(nb)[None, :] < q_blk[:, None]
    gate = jnp.where(past[None, None], gate, NEG)
    n_sel = min(MOBA_TOPK, nb)
    _, sel = lax.top_k(gate, n_sel)
    sel_valid = jnp.arange(n_sel)[None, :] < q_blk[:, None]
    b_ix = jnp.arange(bsz)[:, None, None, None]
    h_ix = jnp.arange(nh)[None, :, None, None]

    def chunk(c):
        start = c * MOBA_QCHUNK
        qc = lax.dynamic_slice_in_dim(q, start, MOBA_QCHUNK, axis=2)
        selc = lax.dynamic_slice_in_dim(sel, start, MOBA_QCHUNK, axis=2)
        validc = lax.dynamic_slice_in_dim(sel_valid, start, MOBA_QCHUNK, axis=0)
        own = start // MOBA_BLOCK
        k_own = lax.dynamic_index_in_dim(k_blocks, own, axis=2, keepdims=False)
        v_own = lax.dynamic_index_in_dim(v_blocks, own, axis=2, keepdims=False)
        k_sel = k_blocks[b_ix, h_ix, selc]
        v_sel = v_blocks[b_ix, h_ix, selc]
        s_past = jnp.einsum('bhqd,bhqskd->bhqsk', qc, k_sel).astype(jnp.float32)
        s_past = jnp.where(validc[None, None, :, :, None], s_past, NEG)
        s_own = jnp.einsum('bhqd,bhkd->bhqk', qc, k_own).astype(jnp.float32)
        q_pos = start + jnp.arange(MOBA_QCHUNK)
        k_pos = own * MOBA_BLOCK + jnp.arange(MOBA_BLOCK)
        s_own = jnp.where(k_pos[None, :] <= q_pos[:, None], s_own, NEG)
        scores = jnp.concatenate([s_past.reshape(bsz, nh, MOBA_QCHUNK, n_sel * MOBA_BLOCK), s_own], axis=-1)
        p = jax.nn.softmax(scores, axis=-1).astype(v.dtype)
        p_past = p[..., :n_sel * MOBA_BLOCK].reshape(bsz, nh, MOBA_QCHUNK, n_sel, MOBA_BLOCK)
        p_own = p[..., n_sel * MOBA_BLOCK:]
        return (jnp.einsum('bhqsk,bhqskd->bhqd', p_past, v_sel)
                + jnp.einsum('bhqk,bhkd->bhqd', p_own, v_own))

    outs = lax.map(chunk, jnp.arange(t // MOBA_QCHUNK))
    return jnp.transpose(outs, (1, 0, 3, 2, 4)).reshape(bsz, t, nh, dh)


def rg_lru(x, w_a, b_a, w_x, b_x, lam):
    bsz, t, w = x.shape
    xb = x.reshape(bsz, t, LRU_BLOCKS, LRU_BLOCK_DIM)
    gate_a = jnp.einsum('btnd,nde->btne', xb, w_a).reshape(bsz, t, w) + b_a
    gate_x = jnp.einsum('btnd,nde->btne', xb, w_x).reshape(bsz, t, w) + b_x
    log_a = -LRU_C * jax.nn.sigmoid(gate_a.astype(jnp.float32)) * jax.nn.softplus(-lam.astype(jnp.float32))
    a = jnp.exp(log_a)
    u = jnp.sqrt(-jnp.expm1(2.0 * log_a)) * jax.nn.sigmoid(gate_x.astype(jnp.float32)) * x.astype(jnp.float32)

    def combine(left, right):
        a_l, u_l = left
        a_r, u_r = right
        return a_l * a_r, a_r * u_l + u_r

    _, h = lax.associative_scan(combine, (a, u), axis=1)
    return h.astype(x.dtype)


def gla_chunked(q, k, v, log_alpha):
    bsz, t, nh, dk = q.shape
    dv = v.shape[-1]
    n = t // GLA_CHUNK

    def to_chunks(z):
        return jnp.moveaxis(z.astype(jnp.float32).reshape(bsz, n, GLA_CHUNK, nh, z.shape[-1]), 1, 0)

    qc, kc, vc, gc = to_chunks(q * (dk ** -0.5)), to_chunks(k), to_chunks(v), to_chunks(log_alpha)
    causal = jnp.tril(jnp.ones((GLA_CHUNK, GLA_CHUNK), dtype=bool))

    def step(s, inp):
        qi, ki, vi, gi = inp
        b = jnp.cumsum(gi, axis=1)
        o_inter = jnp.einsum('bihk,bhkv->bihv', qi * jnp.exp(b), s)
        diff = b[:, :, None] - b[:, None, :]
        decay = jnp.exp(jnp.where(causal[None, :, :, None, None], diff, NEG))
        att = jnp.einsum('bihk,bijhk->bhij', qi, ki[:, None] * decay)
        o_intra = jnp.einsum('bhij,bjhv->bihv', att, vi)
        b_last = b[:, -1]
        s_new = (jnp.exp(b_last)[..., None] * s
                 + jnp.einsum('bjhk,bjhv->bhkv', ki * jnp.exp(b_last[:, None] - b), vi))
        return s_new, o_inter + o_intra

    s0 = jnp.zeros((bsz, nh, dk, dv), jnp.float32)
    _, o = lax.scan(step, s0, (qc, kc, vc, gc))
    return jnp.moveaxis(o, 0, 1).reshape(bsz, t, nh, dv)


def ssd_chunked(x, dt, a_log, bm, cm):
    bsz, t, nh, p = x.shape
    g, n = bm.shape[2], bm.shape[3]
    hg = nh // g
    c = t // SSD_CHUNK
    a = -jnp.exp(a_log.astype(jnp.float32))
    da = (dt * a).reshape(bsz, c, SSD_CHUNK, g, hg)
    xd = (x.astype(jnp.float32) * dt[..., None]).reshape(bsz, c, SSD_CHUNK, g, hg, p)
    bc = bm.astype(jnp.float32).reshape(bsz, c, SSD_CHUNK, g, n)
    cc = cm.astype(jnp.float32).reshape(bsz, c, SSD_CHUNK, g, n)
    cs = jnp.cumsum(da, axis=2)
    seg = cs[:, :, :, None] - cs[:, :, None, :]
    causal = jnp.tril(jnp.ones((SSD_CHUNK, SSD_CHUNK), dtype=bool))
    lmat = jnp.exp(jnp.where(causal[None, None, :, :, None, None], seg, NEG))
    cb = jnp.einsum('bclgn,bcsgn->bclsg', cc, bc)
    y_diag = jnp.einsum('bclsgh,bcsghp->bclghp', cb[..., None] * lmat, xd)
    decay_s = jnp.exp(cs[:, :, -1:] - cs)
    states = jnp.einsum('bcsgn,bcsghp->bcghpn', bc, xd * decay_s[..., None])
    chunk_decay = jnp.exp(cs[:, :, -1])

    def step(s, inp):
        st, dec = inp
        return dec[..., None, None] * s + st, s

    s0 = jnp.zeros((bsz, g, hg, p, n), jnp.float32)
    _, prev = lax.scan(step, s0, (jnp.moveaxis(states, 1, 0), jnp.moveaxis(chunk_decay, 1, 0)))
    prev = jnp.moveaxis(prev, 0, 1)
    y_off = jnp.einsum('bclgn,bcghpn->bclghp', cc, prev) * jnp.exp(cs)[..., None]
    return (y_diag + y_off).reshape(bsz, t, nh, p)


def moe_swiglu(h, router, w1, w3, w2):
    bsz, t, d = h.shape
    hf = h.reshape(bsz * t, d)
    logits = (hf @ router).astype(jnp.float32)
    top_val, top_idx = lax.top_k(logits, TOP_K)
    top_w = jax.nn.softmax(top_val, axis=-1)
    comb = jnp.sum(jax.nn.one_hot(top_idx, N_EXPERTS, dtype=jnp.float32) * top_w[..., None], axis=1).astype(h.dtype)
    out = jnp.zeros_like(hf)
    for e in range(N_EXPERTS):
        out = out + comb[:, e:e + 1] * swiglu(hf, w1[e], w3[e], w2[e])
    return out.reshape(bsz, t, d)


def even_layer(x, norm1, w_in, lru_conv_w, lru_conv_b, lru_w_a, lru_b_a, lru_w_x, lru_b_x,
               lru_lambda, w_out, norm2, ffn_w1, ffn_w3, ffn_w2):
    bsz, t, _ = x.shape
    h = rms_norm(x, norm1)
    proj = h @ w_in
    q, k, v, xr, gr = split_cols(proj, (MOBA_WIDTH, MOBA_WIDTH, MOBA_WIDTH, LRU_WIDTH, LRU_WIDTH))
    heads = lambda z: z.reshape(bsz, t, MOBA_HEADS, MOBA_HEAD_DIM)
    att = moba_attention(heads(q), heads(k), heads(v)).reshape(bsz, t, MOBA_WIDTH)
    xr = causal_dwconv(xr, lru_conv_w, lru_conv_b)
    rec = rg_lru(xr, lru_w_a, lru_b_a, lru_w_x, lru_b_x, lru_lambda) * jax.nn.gelu(gr)
    x = x + jnp.concatenate([att.astype(x.dtype), rec.astype(x.dtype)], axis=-1) @ w_out
    return x + swiglu(rms_norm(x, norm2), ffn_w1, ffn_w3, ffn_w2)


def odd_layer(x, norm1, w_in, gla_w_a2, gla_b_a, gla_norm, ssd_conv_w, ssd_conv_b, ssd_dt_bias,
              ssd_a_log, ssd_d, ssd_norm, w_out, norm2, router, moe_w1, moe_w3, moe_w2):
    bsz, t, _ = x.shape
    h = rms_norm(x, norm1)
    proj = h @ w_in
    q, k, v, r, lr, z, xbc, dt = split_cols(
        proj, (GLA_KEY, GLA_KEY, GLA_VAL, GLA_VAL, GLA_RANK, SSD_WIDTH, SSD_CONV_DIM, SSD_HEADS))
    log_alpha = jax.nn.log_sigmoid((lr @ gla_w_a2 + gla_b_a).astype(jnp.float32)) / GLA_TAU
    gk = lambda zz: zz.reshape(bsz, t, GLA_HEADS, GLA_DK)
    o = gla_chunked(gk(q), gk(k), v.reshape(bsz, t, GLA_HEADS, GLA_DV), gk(log_alpha))
    o = rms_norm(o, gla_norm).astype(x.dtype).reshape(bsz, t, GLA_VAL) * jax.nn.silu(r)
    xbc = jax.nn.silu(causal_dwconv(xbc, ssd_conv_w, ssd_conv_b))
    xs, bm, cm = split_cols(xbc, (SSD_WIDTH, SSD_GROUPS * SSD_STATE, SSD_GROUPS * SSD_STATE))
    dt = jax.nn.softplus(dt.astype(jnp.float32) + ssd_dt_bias.astype(jnp.float32))
    xs_h = xs.reshape(bsz, t, SSD_HEADS, SSD_HEAD_DIM)
    y = ssd_chunked(xs_h, dt, ssd_a_log,
                    bm.reshape(bsz, t, SSD_GROUPS, SSD_STATE), cm.reshape(bsz, t, SSD_GROUPS, SSD_STATE))
    y = y + ssd_d.astype(jnp.float32)[:, None] * xs_h.astype(jnp.float32)
    y = y.reshape(bsz, t, SSD_WIDTH) * jax.nn.silu(z.astype(jnp.float32))
    y = rms_norm(y.reshape(bsz, t, SSD_GROUPS, SSD_WIDTH // SSD_GROUPS),
                 ssd_norm.reshape(SSD_GROUPS, SSD_WIDTH // SSD_GROUPS)).reshape(bsz, t, SSD_WIDTH).astype(x.dtype)
    x = x + jnp.concatenate([o.astype(x.dtype), y], axis=-1) @ w_out
    return x + moe_swiglu(rms_norm(x, norm2), router, moe_w1, moe_w3, moe_w2)


def setup_inputs(seed: int = 0) -> dict:
    key = jax.random.key(seed)
    ks = iter(jax.random.split(key, 48))
    f32 = jnp.float32
    E, O = N_EVEN, N_ODD

    def dense(shape, fan_in):
        return jax.random.normal(next(ks), shape, f32) * (fan_in ** -0.5)

    def gain(shape):
        return 1.0 + 0.02 * jax.random.normal(next(ks), shape, f32)

    def small(shape, scale=0.02):
        return scale * jax.random.normal(next(ks), shape, f32)

    x = jax.random.normal(next(ks), (BATCH, SEQ, D_MODEL), f32)
    inp = {'x': x}
    inp['ev_norm1'] = gain((E, D_MODEL))
    inp['ev_w_in'] = dense((E, D_MODEL, L0_IN), D_MODEL)
    inp['ev_lru_conv_w'] = dense((E, CONV_WIDTH, LRU_WIDTH), CONV_WIDTH)
    inp['ev_lru_conv_b'] = small((E, LRU_WIDTH))
    inp['ev_lru_w_a'] = dense((E, LRU_BLOCKS, LRU_BLOCK_DIM, LRU_BLOCK_DIM), LRU_BLOCK_DIM)
    inp['ev_lru_b_a'] = small((E, LRU_WIDTH))
    inp['ev_lru_w_x'] = dense((E, LRU_BLOCKS, LRU_BLOCK_DIM, LRU_BLOCK_DIM), LRU_BLOCK_DIM)
    inp['ev_lru_b_x'] = small((E, LRU_WIDTH))
    a0 = jax.random.uniform(next(ks), (E, LRU_WIDTH), f32, 0.9, 0.999)
    s = a0 ** (1.0 / LRU_C)
    inp['ev_lru_lambda'] = jnp.log(s) - jnp.log1p(-s)
    inp['ev_w_out'] = dense((E, L0_MIX, D_MODEL), L0_MIX)
    inp['ev_norm2'] = gain((E, D_MODEL))
    inp['ev_ffn_w1'] = dense((E, D_MODEL, FFN_DENSE), D_MODEL)
    inp['ev_ffn_w3'] = dense((E, D_MODEL, FFN_DENSE), D_MODEL)
    inp['ev_ffn_w2'] = dense((E, FFN_DENSE, D_MODEL), FFN_DENSE)
    inp['od_norm1'] = gain((O, D_MODEL))
    inp['od_w_in'] = dense((O, D_MODEL, L1_IN), D_MODEL)
    inp['od_gla_w_a2'] = dense((O, GLA_RANK, GLA_KEY), GLA_RANK)
    inp['od_gla_b_a'] = small((O, GLA_KEY), 0.1)
    inp['od_gla_norm'] = gain((O, GLA_DV))
    inp['od_ssd_conv_w'] = dense((O, CONV_WIDTH, SSD_CONV_DIM), CONV_WIDTH)
    inp['od_ssd_conv_b'] = small((O, SSD_CONV_DIM))
    dt0 = jnp.exp(jax.random.uniform(next(ks), (O, SSD_HEADS), f32, math.log(1e-3), math.log(1e-1)))
    inp['od_ssd_dt_bias'] = dt0 + jnp.log(-jnp.expm1(-dt0))
    inp['od_ssd_a_log'] = jnp.log(jax.random.uniform(next(ks), (O, SSD_HEADS), f32, 1.0, 16.0))
    inp['od_ssd_d'] = gain((O, SSD_HEADS))
    inp['od_ssd_norm'] = gain((O, SSD_WIDTH))
    inp['od_w_out'] = dense((O, L1_MIX, D_MODEL), L1_MIX)
    inp['od_norm2'] = gain((O, D_MODEL))
    inp['od_router'] = dense((O, D_MODEL, N_EXPERTS), D_MODEL)
    inp['od_moe_w1'] = dense((O, N_EXPERTS, D_MODEL, FFN_EXPERT), D_MODEL)
    inp['od_moe_w3'] = dense((O, N_EXPERTS, D_MODEL, FFN_EXPERT), D_MODEL)
    inp['od_moe_w2'] = dense((O, N_EXPERTS, FFN_EXPERT, D_MODEL), FFN_EXPERT)
    inp['final_norm'] = gain((D_MODEL,))
    return inp


def reference(x, ev_norm1, ev_w_in, ev_lru_conv_w, ev_lru_conv_b, ev_lru_w_a, ev_lru_b_a, ev_lru_w_x,
              ev_lru_b_x, ev_lru_lambda, ev_w_out, ev_norm2, ev_ffn_w1, ev_ffn_w3, ev_ffn_w2,
              od_norm1, od_w_in, od_gla_w_a2, od_gla_b_a, od_gla_norm, od_ssd_conv_w, od_ssd_conv_b,
              od_ssd_dt_bias, od_ssd_a_log, od_ssd_d, od_ssd_norm, od_w_out, od_norm2, od_router,
              od_moe_w1, od_moe_w3, od_moe_w2, final_norm):
    for layer in range(DEPTH):
        i = layer // 2
        if layer % 2 == 0:
            x = even_layer(x, ev_norm1[i], ev_w_in[i], ev_lru_conv_w[i], ev_lru_conv_b[i], ev_lru_w_a[i],
                           ev_lru_b_a[i], ev_lru_w_x[i], ev_lru_b_x[i], ev_lru_lambda[i], ev_w_out[i],
                           ev_norm2[i], ev_ffn_w1[i], ev_ffn_w3[i], ev_ffn_w2[i])
        else:
            x = odd_layer(x, od_norm1[i], od_w_in[i], od_gla_w_a2[i], od_gla_b_a[i], od_gla_norm[i],
                          od_ssd_conv_w[i], od_ssd_conv_b[i], od_ssd_dt_bias[i], od_ssd_a_log[i], od_ssd_d[i],
                          od_ssd_norm[i], od_w_out[i], od_norm2[i], od_router[i], od_moe_w1[i],
                          od_moe_w3[i], od_moe_w2[i])
    return rms_norm(x, final_norm)
```

```python
import functools
import math

import jax
import jax.numpy as jnp
from jax import lax
from jax.experimental import pallas as pl
from jax.experimental.pallas import tpu as pltpu

F32 = jnp.float32
BF16 = jnp.bfloat16
HIGHEST = lax.Precision.HIGHEST

NORM_EPS = 1e-6
NEG = -1e30
LANES = 128
VMEM_LIMIT = 56 * 1024 * 1024

CONV_WIDTH = 4
CONV_PAD = 8
MOBA_BLOCK = 256
MOBA_TOPK = 3
LRU_C = 8.0
GLA_TAU = 16.0
GLA_SUB = 16
N_EXPERTS = 8
MOE_ROW_TILE = 256


def _nt(a, b, precision=None):
    return lax.dot_general(a, b, (((1,), (1,)), ((), ())), precision=precision,
                           preferred_element_type=F32)


def _sigmoid(x):
    return 1.0 / (1.0 + jnp.exp(-x))


def _silu(x):
    return x * _sigmoid(x)


def _softplus(x):
    return jnp.maximum(x, 0.0) + jnp.log(1.0 + jnp.exp(-jnp.abs(x)))


def _params(*sem):
    return pltpu.CompilerParams(dimension_semantics=sem, vmem_limit_bytes=VMEM_LIMIT)


def _rmsnorm_kernel(x_ref, g_ref, o_ref):
    x = x_ref[...]
    y = x * lax.rsqrt(jnp.mean(x * x, axis=-1, keepdims=True) + NORM_EPS)
    o_ref[...] = (y * g_ref[...]).astype(o_ref.dtype)


def rmsnorm(x, g, out_dtype=BF16, tm=256):
    m, d = x.shape
    return pl.pallas_call(
        _rmsnorm_kernel,
        grid=(m // tm,),
        in_specs=[pl.BlockSpec((tm, d), lambda i: (i, 0)),
                  pl.BlockSpec((1, d), lambda i: (0, 0))],
        out_specs=pl.BlockSpec((tm, d), lambda i: (i, 0)),
        out_shape=jax.ShapeDtypeStruct((m, d), out_dtype),
        compiler_params=_params("parallel"),
        name="rmsnorm",
    )(x, g.reshape(1, d))


def _mm_kernel(*refs, n_a, has_res, act_pair):
    a_refs = refs[:n_a]
    n_w = 2 if act_pair else 1
    w_refs = refs[n_a:n_a + n_w]
    pos = n_a + n_w
    r_ref = refs[pos] if has_res else None
    pos += int(has_res)
    o_ref = refs[pos]
    wb_refs = refs[pos + 1:pos + 1 + n_w]

    @pl.when(pl.program_id(1) == 0)
    def _():
        for w_ref, wb_ref in zip(w_refs, wb_refs):
            wb_ref[...] = w_ref[...].astype(BF16)

    def contract(wb_ref):
        acc = None
        k0 = 0
        for a_ref in a_refs:
            ka = a_ref.shape[1]
            part = jnp.dot(a_ref[...], wb_ref[k0:k0 + ka, :], preferred_element_type=F32)
            acc = part if acc is None else acc + part
            k0 += ka
        return acc

    if act_pair:
        out = _silu(contract(wb_refs[0])) * contract(wb_refs[1])
    else:
        out = contract(wb_refs[0])
    if has_res:
        out = out + r_ref[...]
    o_ref[...] = out.astype(o_ref.dtype)


def matmul(a_list, w_list, *, w_col0=0, n_cols=None, res=None, out_dtype=F32, tm=512, tn=512,
           name="matmul"):
    m = a_list[0].shape[0]
    k = sum(a.shape[1] for a in a_list)
    assert all(w.shape[0] == k for w in w_list)
    n_cols = w_list[0].shape[1] - w_col0 if n_cols is None else n_cols
    assert m % tm == 0 and n_cols % tn == 0 and w_col0 % tn == 0
    nb0 = w_col0 // tn
    in_specs = [pl.BlockSpec((tm, a.shape[1]), lambda n, i: (i, 0)) for a in a_list]
    in_specs += [pl.BlockSpec((k, tn), lambda n, i: (0, n + nb0)) for _ in w_list]
    args = list(a_list) + list(w_list)
    if res is not None:
        in_specs.append(pl.BlockSpec((tm, tn), lambda n, i: (i, n)))
        args.append(res)
    kern = functools.partial(_mm_kernel, n_a=len(a_list), has_res=res is not None,
                             act_pair=len(w_list) == 2)
    return pl.pallas_call(
        kern,
        grid=(n_cols // tn, m // tm),
        in_specs=in_specs,
        out_specs=pl.BlockSpec((tm, tn), lambda n, i: (i, n)),
        out_shape=jax.ShapeDtypeStruct((m, n_cols), out_dtype),
        scratch_shapes=[pltpu.VMEM((k, tn), BF16) for _ in w_list],
        compiler_params=_params("parallel", "arbitrary"),
        name=name,
    )(*args)


def _mm_ktiled_kernel(a_ref, w_ref, r_ref, o_ref):
    @pl.when(pl.program_id(2) == 0)
    def _():
        o_ref[...] = r_ref[...]

    o_ref[...] += jnp.dot(a_ref[...], w_ref[...].astype(BF16), preferred_element_type=F32)


def matmul_ktiled_res(a, w, res, *, tm=1024, tn=1024, tk=1024, name="matmul_ktiled"):
    m, k = a.shape
    n = w.shape[1]
    assert m % tm == 0 and n % tn == 0 and k % tk == 0
    return pl.pallas_call(
        _mm_ktiled_kernel,
        grid=(m // tm, n // tn, k // tk),
        in_specs=[pl.BlockSpec((tm, tk), lambda i, j, kk: (i, kk)),
                  pl.BlockSpec((tk, tn), lambda i, j, kk: (kk, j)),
                  pl.BlockSpec((tm, tn), lambda i, j, kk: (i, j))],
        out_specs=pl.BlockSpec((tm, tn), lambda i, j, kk: (i, j)),
        out_shape=jax.ShapeDtypeStruct((m, n), F32),
        compiler_params=_params("parallel", "parallel", "arbitrary"),
        name=name,
    )(a, w, res)


def _moba_kernel(q_ref, k_ref, v_ref, o_ref, kmean_ref, sel_ref, m_ref, l_ref, acc_ref, *,
                 blk, nblk, topk, scale):
    qb = pl.program_id(2)

    @pl.when(qb == 0)
    def _():
        kmean_ref[...] = jnp.zeros_like(kmean_ref)
        for n in range(nblk):
            kb = k_ref[n * blk:(n + 1) * blk, :].astype(F32)
            kmean_ref[n:n + 1, :] = jnp.mean(kb, axis=0, keepdims=True)

    q = q_ref[...]
    gate = _nt(q.astype(F32) * scale, kmean_ref[...], precision=HIGHEST)
    lane = lax.broadcasted_iota(jnp.int32, gate.shape, 1)
    t = jnp.where(lane < qb, gate, -jnp.inf)
    sel = jnp.zeros(gate.shape, F32)
    for _ in range(topk):
        mx = jnp.max(t, axis=-1, keepdims=True)
        idx = jnp.min(jnp.where(t == mx, lane, LANES), axis=-1, keepdims=True)
        pick = (lane == idx) & (mx > -jnp.inf)
        sel = jnp.where(pick, 1.0, sel)
        t = jnp.where(lane == idx, -jnp.inf, t)
    sel_ref[...] = sel

    row = lax.broadcasted_iota(jnp.int32, (blk, blk), 0)
    col = lax.broadcasted_iota(jnp.int32, (blk, blk), 1)
    own = pl.multiple_of(qb * blk, blk)
    s = _nt(q, k_ref[pl.ds(own, blk), :]) * scale
    s = jnp.where(col <= row, s, NEG)
    m0 = jnp.max(s, axis=-1, keepdims=True)
    p = jnp.exp(s - m0)
    m_ref[...] = m0
    l_ref[...] = jnp.sum(p, axis=-1, keepdims=True)
    acc_ref[...] = jnp.dot(p.astype(BF16), v_ref[pl.ds(own, blk), :], preferred_element_type=F32)

    for kb in range(nblk - 1):
        @pl.when(kb < qb)
        def _(kb=kb):
            chosen = sel_ref[:, kb:kb + 1] > 0.5
            sp = _nt(q, k_ref[kb * blk:(kb + 1) * blk, :]) * scale
            sp = jnp.where(chosen, sp, NEG)
            m_old = m_ref[...]
            m_new = jnp.maximum(m_old, jnp.max(sp, axis=-1, keepdims=True))
            alpha = jnp.exp(m_old - m_new)
            pp = jnp.exp(sp - m_new)
            m_ref[...] = m_new
            l_ref[...] = alpha * l_ref[...] + jnp.sum(pp, axis=-1, keepdims=True)
            acc_ref[...] = alpha * acc_ref[...] + jnp.dot(
                pp.astype(BF16), v_ref[kb * blk:(kb + 1) * blk, :], preferred_element_type=F32)

    o_ref[...] = (acc_ref[...] / l_ref[...]).astype(o_ref.dtype)


def moba_attention(qkv, *, nh, dh):
    bsz, t, _ = qkv.shape
    blk = MOBA_BLOCK
    nblk = t // blk
    assert t % blk == 0 and nblk <= LANES and dh % LANES == 0
    kern = functools.partial(_moba_kernel, blk=blk, nblk=nblk, topk=min(MOBA_TOPK, nblk),
                             scale=dh ** -0.5)
    return pl.pallas_call(
        kern,
        grid=(bsz, nh, nblk),
        in_specs=[pl.BlockSpec((None, blk, dh), lambda b, h, i: (b, i, h)),
                  pl.BlockSpec((None, t, dh), lambda b, h, i: (b, 0, nh + h)),
                  pl.BlockSpec((None, t, dh), lambda b, h, i: (b, 0, 2 * nh + h))],
        out_specs=pl.BlockSpec((None, blk, dh), lambda b, h, i: (b, i, h)),
        out_shape=jax.ShapeDtypeStruct((bsz, t, nh * dh), BF16),
        scratch_shapes=[pltpu.VMEM((LANES, dh), F32),
                        pltpu.VMEM((blk, LANES), F32),
                        pltpu.VMEM((blk, 1), F32),
                        pltpu.VMEM((blk, 1), F32),
                        pltpu.VMEM((blk, dh), F32)],
        compiler_params=_params("parallel", "parallel", "arbitrary"),
        name="moba_attention",
    )(qkv, qkv, qkv)


def _causal_conv(x, buf_ref, w_ref, b_ref, first):
    rows = x.shape[0]

    @pl.when(first)
    def _():
        buf_ref[0:CONV_PAD, :] = jnp.zeros((CONV_PAD, x.shape[1]), F32)

    buf_ref[CONV_PAD:CONV_PAD + rows, :] = x
    y = b_ref[...] + w_ref[CONV_WIDTH - 1:CONV_WIDTH, :] * x
    for kk in range(CONV_WIDTH - 1):
        off = CONV_PAD - (CONV_WIDTH - 1) + kk
        y = y + w_ref[kk:kk + 1, :] * buf_ref[off:off + rows, :]
    buf_ref[0:CONV_PAD, :] = x[rows - CONV_PAD:rows, :]
    return y


def _lru_kernel(xr_ref, gr_ref, cw_ref, cb_ref, wa_ref, ba_ref, wx_ref, bx_ref, lam_ref, o_ref,
                xbuf_ref, h_ref, *, nblocks, bdim):
    ti = pl.program_id(2)
    rows = xr_ref.shape[0]

    @pl.when(ti == 0)
    def _():
        h_ref[...] = jnp.zeros_like(h_ref)

    x = _causal_conv(xr_ref[...], xbuf_ref, cw_ref, cb_ref, ti == 0)
    ga, gx = [], []
    for n in range(nblocks):
        xb = x[:, n * bdim:(n + 1) * bdim].astype(BF16)
        ga.append(jnp.dot(xb, wa_ref[n].astype(BF16), preferred_element_type=F32))
        gx.append(jnp.dot(xb, wx_ref[n].astype(BF16), preferred_element_type=F32))
    gate_a = jnp.concatenate(ga, axis=-1) + ba_ref[...]
    gate_x = jnp.concatenate(gx, axis=-1) + bx_ref[...]
    log_a = -LRU_C * _sigmoid(gate_a) * _softplus(-lam_ref[...])
    a = jnp.exp(log_a)
    u = jnp.sqrt(1.0 - jnp.exp(2.0 * log_a)) * _sigmoid(gate_x) * x

    rid = lax.broadcasted_iota(jnp.int32, a.shape, 0)
    s = 1
    while s < rows:
        keep = rid >= s
        a_prev = jnp.where(keep, pltpu.roll(a, s, 0), 1.0)
        u_prev = jnp.where(keep, pltpu.roll(u, s, 0), 0.0)
        u = a * u_prev + u
        a = a * a_prev
        s *= 2
    h = u + a * h_ref[0:1, :]
    h_ref[0:1, :] = h[rows - 1:rows, :]

    g = gr_ref[...]
    gelu = 0.5 * g * (1.0 + jnp.tanh(math.sqrt(2.0 / math.pi) * (g + 0.044715 * g * g * g)))
    o_ref[...] = (h * gelu).astype(o_ref.dtype)


def conv_rglru(xg, conv_w, conv_b, w_a, b_a, w_x, b_x, lam, *, tt=256, tc=512):
    bsz, t, w2 = xg.shape
    w = w2 // 2
    nblocks_all, bdim, _ = w_a.shape
    nb = tc // bdim
    assert t % tt == 0 and w % tc == 0 and tc % bdim == 0
    row = lambda v: v.reshape(1, w)
    vec_spec = pl.BlockSpec((1, tc), lambda b, c, i: (0, c))
    gate_spec = pl.BlockSpec((nb, bdim, bdim), lambda b, c, i: (c, 0, 0))
    kern = functools.partial(_lru_kernel, nblocks=nb, bdim=bdim)
    return pl.pallas_call(
        kern,
        grid=(bsz, w // tc, t // tt),
        in_specs=[pl.BlockSpec((None, tt, tc), lambda b, c, i: (b, i, c)),
                  pl.BlockSpec((None, tt, tc), lambda b, c, i: (b, i, w // tc + c)),
                  pl.BlockSpec((CONV_WIDTH, tc), lambda b, c, i: (0, c)),
                  vec_spec, gate_spec, vec_spec, gate_spec, vec_spec, vec_spec],
        out_specs=pl.BlockSpec((None, tt, tc), lambda b, c, i: (b, i, c)),
        out_shape=jax.ShapeDtypeStruct((bsz, t, w), BF16),
        scratch_shapes=[pltpu.VMEM((CONV_PAD + tt, tc), F32),
                        pltpu.VMEM((8, tc), F32)],
        compiler_params=_params("parallel", "parallel", "arbitrary"),
        name="conv_rglru",
    )(xg, xg, conv_w, row(conv_b), w_a, row(b_a), w_x, row(b_x), row(lam))


def _gla_kernel(q_ref, k_ref, v_ref, r_ref, lr_ref, wa_ref, ba_ref, g_ref, o_ref, st_ref, *,
                chunk, scale):
    ci = pl.program_id(2)

    @pl.when(ci == 0)
    def _():
        st_ref[...] = jnp.zeros_like(st_ref)

    dk = q_ref.shape[1]
    z = jnp.dot(lr_ref[...], wa_ref[...], precision=HIGHEST, preferred_element_type=F32) + ba_ref[...]
    g = (jnp.minimum(z, 0.0) - jnp.log(1.0 + jnp.exp(-jnp.abs(z)))) / GLA_TAU
    row = lax.broadcasted_iota(jnp.int32, (chunk, chunk), 0)
    col = lax.broadcasted_iota(jnp.int32, (chunk, chunk), 1)
    tril = jnp.where(col <= row, 1.0, 0.0).astype(F32)
    b = jnp.dot(tril, g, precision=HIGHEST, preferred_element_type=F32)
    sub0 = row - jnp.bitwise_and(row, GLA_SUB - 1)
    b_sub = jnp.dot(jnp.where(col < sub0, 1.0, 0.0).astype(F32), g, precision=HIGHEST,
                    preferred_element_type=F32)
    q = q_ref[...].astype(F32) * scale
    k = k_ref[...].astype(F32)
    v = v_ref[...]
    b_last = b[chunk - 1:chunk, :]

    o = _nt((q * jnp.exp(b)).astype(BF16), st_ref[...].astype(BF16))

    jrow = lax.broadcasted_iota(jnp.int32, (chunk, dk), 0)
    srow = lax.broadcasted_iota(jnp.int32, (GLA_SUB, chunk), 0)
    scol = lax.broadcasted_iota(jnp.int32, (GLA_SUB, chunk), 1)
    parts = []
    for i in range(chunk // GLA_SUB):
        lo, hi = i * GLA_SUB, (i + 1) * GLA_SUB
        ref = b_sub[lo:lo + 1, :]
        qi = (q[lo:hi] * jnp.exp(b[lo:hi] - ref)).astype(BF16)
        ki = (k * jnp.exp(jnp.where(jrow < hi, ref - b, 0.0))).astype(BF16)
        att = _nt(qi, ki)
        att = jnp.where(scol <= srow + lo, att, 0.0)
        parts.append(jnp.dot(att.astype(BF16), v, preferred_element_type=F32))
    o = o + jnp.concatenate(parts, axis=0)

    kd = (k * jnp.exp(b_last - b)).astype(BF16)
    upd = lax.dot_general(v, kd, (((0,), (0,)), ((), ())), preferred_element_type=F32)
    st_ref[...] = st_ref[...] * jnp.exp(b_last) + upd

    y = o * lax.rsqrt(jnp.mean(o * o, axis=-1, keepdims=True) + NORM_EPS) * g_ref[...]
    o_ref[...] = (y * _silu(r_ref[...].astype(F32))).astype(o_ref.dtype)


def gla(p1a, sm, w_a2p, b_a, norm, *, nh, dk, dv, chunk=128):
    bsz, t, _ = p1a.shape
    assert t % chunk == 0 and chunk % GLA_SUB == 0
    kq = nh * dk
    voff = 2 * kq // dv
    roff = voff + nh
    kern = functools.partial(_gla_kernel, chunk=chunk, scale=dk ** -0.5)
    return pl.pallas_call(
        kern,
        grid=(bsz, nh, t // chunk),
        in_specs=[pl.BlockSpec((None, chunk, dk), lambda b, h, c: (b, c, h)),
                  pl.BlockSpec((None, chunk, dk), lambda b, h, c: (b, c, nh + h)),
                  pl.BlockSpec((None, chunk, dv), lambda b, h, c: (b, c, voff + h)),
                  pl.BlockSpec((None, chunk, dv), lambda b, h, c: (b, c, roff + h)),
                  pl.BlockSpec((None, chunk, LANES), lambda b, h, c: (b, c, 0)),
                  pl.BlockSpec((LANES, dk), lambda b, h, c: (0, h)),
                  pl.BlockSpec((1, dk), lambda b, h, c: (0, h)),
                  pl.BlockSpec((1, dv), lambda b, h, c: (0, 0))],
        out_specs=pl.BlockSpec((None, chunk, dv), lambda b, h, c: (b, c, h)),
        out_shape=jax.ShapeDtypeStruct((bsz, t, nh * dv), BF16),
        scratch_shapes=[pltpu.VMEM((dv, dk), F32)],
        compiler_params=_params("parallel", "parallel", "arbitrary"),
        name="gla",
    )(p1a, p1a, p1a, p1a, sm, w_a2p, b_a.reshape(1, kq), norm.reshape(1, dv))


def _ssd_kernel(z_ref, xs_ref, bm_ref, cm_ref, dt_ref, cwx_ref, cwb_ref, cwc_ref, cbx_ref, cbb_ref,
                cbc_ref, dtb_ref, alog_ref, dfull_ref, norm_ref, o_ref,
                xbuf_ref, bbuf_ref, cbuf_ref, st_ref, *, chunk, hpg, hd):
    ci = pl.program_id(2)
    first = ci == 0

    @pl.when(first)
    def _():
        st_ref[...] = jnp.zeros_like(st_ref)

    xs = _silu(_causal_conv(xs_ref[...], xbuf_ref, cwx_ref, cbx_ref, first))
    bm = _silu(_causal_conv(bm_ref[...], bbuf_ref, cwb_ref, cbb_ref, first))
    cm = _silu(_causal_conv(cm_ref[...], cbuf_ref, cwc_ref, cbc_ref, first))
    width = hpg * hd

    dt = _softplus(dt_ref[...] + dtb_ref[...])
    da = dt * (-jnp.exp(alog_ref[...]))
    row = lax.broadcasted_iota(jnp.int32, (chunk, chunk), 0)
    col = lax.broadcasted_iota(jnp.int32, (chunk, chunk), 1)
    causal = col <= row
    tril = jnp.where(causal, 1.0, 0.0).astype(F32)
    cs = jnp.dot(tril, da, precision=HIGHEST, preferred_element_type=F32)
    cs_t = cs.T

    erow = lax.broadcasted_iota(jnp.int32, (LANES, width), 0)
    ecol = lax.broadcasted_iota(jnp.int32, (LANES, width), 1)
    expand = jnp.where((ecol >= erow * hd) & (ecol < (erow + 1) * hd), 1.0, 0.0).astype(F32)
    cs_full = jnp.dot(cs, expand, precision=HIGHEST, preferred_element_type=F32)
    dt_full = jnp.dot(dt, expand, precision=HIGHEST, preferred_element_type=F32)
    cs_last = cs_full[chunk - 1:chunk, :]

    xd = xs * dt_full
    xd_b = xd.astype(BF16)
    bm_b = bm.astype(BF16)
    cm_b = cm.astype(BF16)
    cb = _nt(cm_b, bm_b)

    lane = lax.broadcasted_iota(jnp.int32, (chunk, 2 * hd), 1)
    slabs = []
    for j in range(hpg // 2):
        xpair = xd_b[:, 2 * j * hd:(2 * j + 2) * hd]
        acc = None
        for half in range(2):
            h = 2 * j + half
            seg = cs[:, h:h + 1] - cs_t[h:h + 1, :]
            lmat = jnp.exp(jnp.where(causal, seg, NEG))
            mh = (cb * lmat).astype(BF16)
            mine = (lane >= half * hd) & (lane < (half + 1) * hd)
            part = jnp.dot(mh, jnp.where(mine, xpair, jnp.zeros_like(xpair)),
                           preferred_element_type=F32)
            acc = part if acc is None else acc + part
        slabs.append(acc)
    y = jnp.concatenate(slabs, axis=-1)

    st = st_ref[...]
    y = y + jnp.dot(cm_b, st.astype(BF16), preferred_element_type=F32) * jnp.exp(cs_full)
    xdd = (xd * jnp.exp(cs_last - cs_full)).astype(BF16)
    upd = lax.dot_general(bm_b, xdd, (((0,), (0,)), ((), ())), preferred_element_type=F32)
    st_ref[...] = st * jnp.exp(cs_last) + upd

    y = y + dfull_ref[...] * xs
    y = y * _silu(z_ref[...])
    y = y * lax.rsqrt(jnp.mean(y * y, axis=-1, keepdims=True) + NORM_EPS) * norm_ref[...]
    o_ref[...] = y.astype(o_ref.dtype)


def ssd(p1b, sm, conv_w, conv_b, dt_bias, a_log, d_skip, norm, *, nheads, hd, groups, nstate,
        chunk=256):
    bsz, t, _ = p1b.shape
    width = nheads * hd
    gw = width // groups
    hpg = nheads // groups
    assert t % chunk == 0 and nstate == LANES and gw % LANES == 0 and hpg % 2 == 0
    xoff = width // gw
    boff = 2 * width // nstate
    coff = boff + groups
    cxo, cbo, cco = 0, width // nstate, width // nstate + groups
    padl = lambda v: jnp.pad(v.reshape(groups, 1, hpg), ((0, 0), (0, 0), (0, LANES - hpg)))
    dfull = jnp.repeat(d_skip.reshape(groups, hpg), hd, axis=1).reshape(groups, 1, gw)
    cb2 = conv_b.reshape(1, -1)
    g3 = lambda b, g, c: (g, 0, 0)
    kern = functools.partial(_ssd_kernel, chunk=chunk, hpg=hpg, hd=hd)
    return pl.pallas_call(
        kern,
        grid=(bsz, groups, t // chunk),
        in_specs=[pl.BlockSpec((None, chunk, gw), lambda b, g, c: (b, c, g)),
                  pl.BlockSpec((None, chunk, gw), lambda b, g, c: (b, c, xoff + g)),
                  pl.BlockSpec((None, chunk, nstate), lambda b, g, c: (b, c, boff + g)),
                  pl.BlockSpec((None, chunk, nstate), lambda b, g, c: (b, c, coff + g)),
                  pl.BlockSpec((None, chunk, LANES), lambda b, g, c: (b, c, 1 + g)),
                  pl.BlockSpec((CONV_WIDTH, gw), lambda b, g, c: (0, cxo + g)),
                  pl.BlockSpec((CONV_WIDTH, nstate), lambda b, g, c: (0, cbo + g)),
                  pl.BlockSpec((CONV_WIDTH, nstate), lambda b, g, c: (0, cco + g)),
                  pl.BlockSpec((1, gw), lambda b, g, c: (0, cxo + g)),
                  pl.BlockSpec((1, nstate), lambda b, g, c: (0, cbo + g)),
                  pl.BlockSpec((1, nstate), lambda b, g, c: (0, cco + g)),
                  pl.BlockSpec((None, 1, LANES), g3),
                  pl.BlockSpec((None, 1, LANES), g3),
                  pl.BlockSpec((None, 1, gw), g3),
                  pl.BlockSpec((1, gw), lambda b, g, c: (0, g))],
        out_specs=pl.BlockSpec((None, chunk, gw), lambda b, g, c: (b, c, g)),
        out_shape=jax.ShapeDtypeStruct((bsz, t, width), BF16),
        scratch_shapes=[pltpu.VMEM((CONV_PAD + chunk, gw), F32),
                        pltpu.VMEM((CONV_PAD + chunk, nstate), F32),
                        pltpu.VMEM((CONV_PAD + chunk, nstate), F32),
                        pltpu.VMEM((nstate, gw), F32)],
        compiler_params=_params("parallel", "parallel", "arbitrary"),
        name="ssd",
    )(p1b, p1b, p1b, p1b, sm, conv_w, conv_w, conv_w, cb2, cb2, cb2,
      padl(dt_bias), padl(a_log), dfull, norm.reshape(1, width))


def _norm_router_kernel(x_ref, g_ref, r_ref, h_ref, route_ref, *, n_exp):
    x = x_ref[...]
    y = x * lax.rsqrt(jnp.mean(x * x, axis=-1, keepdims=True) + NORM_EPS) * g_ref[...]
    h_ref[...] = y
    logits = jnp.dot(y, r_ref[...], precision=HIGHEST, preferred_element_type=F32)
    lane = lax.broadcasted_iota(jnp.int32, logits.shape, 1)
    t = jnp.where(lane < n_exp, logits, -jnp.inf)
    m1 = jnp.max(t, axis=-1, keepdims=True)
    i1 = jnp.min(jnp.where(t == m1, lane, LANES), axis=-1, keepdims=True)
    t2 = jnp.where(lane == i1, -jnp.inf, t)
    m2 = jnp.max(t2, axis=-1, keepdims=True)
    i2 = jnp.min(jnp.where(t2 == m2, lane, LANES), axis=-1, keepdims=True)
    e = jnp.exp(m2 - m1)
    w1 = 1.0 / (1.0 + e)
    w2 = e / (1.0 + e)
    route = jnp.where(lane == 0, i1.astype(F32),
                      jnp.where(lane == 1, i2.astype(F32),
                                jnp.where(lane == 2, w1, jnp.where(lane == 3, w2, 0.0))))
    route_ref[...] = route


def norm_router(x, g, router, tm=256):
    m, d = x.shape
    n_exp = router.shape[1]
    rp = jnp.pad(router, ((0, 0), (0, LANES - n_exp)))
    return pl.pallas_call(
        functools.partial(_norm_router_kernel, n_exp=n_exp),
        grid=(m // tm,),
        in_specs=[pl.BlockSpec((tm, d), lambda i: (i, 0)),
                  pl.BlockSpec((1, d), lambda i: (0, 0)),
                  pl.BlockSpec((d, LANES), lambda i: (0, 0))],
        out_specs=[pl.BlockSpec((tm, d), lambda i: (i, 0)),
                   pl.BlockSpec((tm, LANES), lambda i: (i, 0))],
        out_shape=[jax.ShapeDtypeStruct((m, d), F32), jax.ShapeDtypeStruct((m, LANES), F32)],
        compiler_params=_params("parallel"),
        name="norm_router",
    )(x, g.reshape(1, d), rp)


def _row_gather(idx_ref, base, src_hbm, dst_ref, sem, rows):
    def copy(r):
        return pltpu.make_async_copy(src_hbm.at[pl.ds(idx_ref[base + r], 1)],
                                     dst_ref.at[pl.ds(r, 1)], sem)

    def start(r, c):
        copy(r).start()
        return c

    def wait(r, c):
        copy(r).wait()
        return c

    lax.fori_loop(0, rows, start, 0)
    lax.fori_loop(0, rows, wait, 0)


def _dispatch_kernel(tok_ref, h_hbm, o_ref, buf_ref, sem):
    rows = buf_ref.shape[0]
    _row_gather(tok_ref, pl.program_id(0) * rows, h_hbm, buf_ref, sem, rows)
    o_ref[...] = buf_ref[...].astype(o_ref.dtype)


def moe_dispatch(row_token, h, tm=MOE_ROW_TILE):
    r = row_token.shape[0]
    d = h.shape[1]
    return pl.pallas_call(
        _dispatch_kernel,
        grid_spec=pltpu.PrefetchScalarGridSpec(
            num_scalar_prefetch=1,
            grid=(r // tm,),
            in_specs=[pl.BlockSpec(memory_space=pl.ANY)],
            out_specs=pl.BlockSpec((tm, d), lambda i, tok: (i, 0)),
            scratch_shapes=[pltpu.VMEM((tm, d), F32), pltpu.SemaphoreType.DMA(())]),
        out_shape=jax.ShapeDtypeStruct((r, d), BF16),
        compiler_params=_params("arbitrary"),
        name="moe_dispatch",
    )(row_token, h)


def _moe_mm_kernel(te_ref, tv_ref, x_ref, *refs, act_pair):
    n_w = 2 if act_pair else 1
    w_refs = refs[:n_w]
    o_ref = refs[n_w]
    wb_refs = refs[n_w + 1:]
    r = pl.program_id(1)

    @pl.when((r == 0) | (te_ref[r] != te_ref[jnp.maximum(r - 1, 0)]))
    def _():
        for w_ref, wb_ref in zip(w_refs, wb_refs):
            wb_ref[...] = w_ref[...].astype(BF16)

    @pl.when(tv_ref[r] > 0)
    def _():
        x = x_ref[...]
        out = jnp.dot(x, wb_refs[0][...], preferred_element_type=F32)
        if act_pair:
            out = _silu(out) * jnp.dot(x, wb_refs[1][...], preferred_element_type=F32)
        o_ref[...] = out.astype(o_ref.dtype)

    @pl.when(tv_ref[r] == 0)
    def _():
        o_ref[...] = jnp.zeros_like(o_ref)


def moe_matmul(tile_expert, tile_valid, x, w_list, *, out_dtype, tm=MOE_ROW_TILE, tn=256,
               name="moe_matmul"):
    r, k = x.shape
    n = w_list[0].shape[2]
    assert r % tm == 0 and n % tn == 0
    kern = functools.partial(_moe_mm_kernel, act_pair=len(w_list) == 2)
    w_spec = pl.BlockSpec((None, k, tn), lambda j, i, te, tv: (te[i], 0, j))
    return pl.pallas_call(
        kern,
        grid_spec=pltpu.PrefetchScalarGridSpec(
            num_scalar_prefetch=2,
            grid=(n // tn, r // tm),
            in_specs=[pl.BlockSpec((tm, k), lambda j, i, te, tv: (i, 0))] + [w_spec] * len(w_list),
            out_specs=pl.BlockSpec((tm, tn), lambda j, i, te, tv: (i, j)),
            scratch_shapes=[pltpu.VMEM((k, tn), BF16) for _ in w_list]),
        out_shape=jax.ShapeDtypeStruct((r, n), out_dtype),
        compiler_params=_params("parallel", "arbitrary"),
        name=name,
    )(tile_expert, tile_valid, x, *w_list)


def _combine_kernel(d0_ref, d1_ref, x_ref, route_ref, g_ref, y_hbm, o_ref, buf0_ref, buf1_ref, sems):
    rows = x_ref.shape[0]
    base = pl.program_id(0) * rows
    _row_gather(d0_ref, base, y_hbm, buf0_ref, sems.at[0], rows)
    _row_gather(d1_ref, base, y_hbm, buf1_ref, sems.at[1], rows)
    w0 = route_ref[:, 2:3]
    w1 = route_ref[:, 3:4]
    x = x_ref[...] + w0 * buf0_ref[...] + w1 * buf1_ref[...]
    y = x * lax.rsqrt(jnp.mean(x * x, axis=-1, keepdims=True) + NORM_EPS)
    o_ref[...] = y * g_ref[...]


def moe_combine_norm(dest0, dest1, x, route, y, g, tm=256):
    m, d = x.shape
    return pl.pallas_call(
        _combine_kernel,
        grid_spec=pltpu.PrefetchScalarGridSpec(
            num_scalar_prefetch=2,
            grid=(m // tm,),
            in_specs=[pl.BlockSpec((tm, d), lambda i, a, b: (i, 0)),
                      pl.BlockSpec((tm, LANES), lambda i, a, b: (i, 0)),
                      pl.BlockSpec((1, d), lambda i, a, b: (0, 0)),
                      pl.BlockSpec(memory_space=pl.ANY)],
            out_specs=pl.BlockSpec((tm, d), lambda i, a, b: (i, 0)),
            scratch_shapes=[pltpu.VMEM((tm, d), F32), pltpu.VMEM((tm, d), F32),
                            pltpu.SemaphoreType.DMA((2,))]),
        out_shape=jax.ShapeDtypeStruct((m, d), F32),
        compiler_params=_params("arbitrary"),
        name="moe_combine_norm",
    )(dest0, dest1, x, route, g.reshape(1, d), y)


def _moe_plan(ids, tm):
    n_tok = ids.shape[0]
    e_flat = ids.reshape(-1)
    onehot = (e_flat[:, None] == jnp.arange(N_EXPERTS, dtype=jnp.int32)[None, :]).astype(jnp.int32)
    csum = jnp.cumsum(onehot, axis=0)
    pos = jnp.sum(csum * onehot, axis=1) - 1
    counts = csum[-1]
    padded = ((counts + tm - 1) // tm) * tm
    gend = jnp.cumsum(padded)
    gstart = gend - padded
    dest = gstart[e_flat] + pos
    n_rows = 2 * n_tok + N_EXPERTS * tm
    row_token = jnp.zeros((n_rows,), jnp.int32).at[dest].set(
        jnp.arange(2 * n_tok, dtype=jnp.int32) // 2)
    tile_start = jnp.arange(n_rows // tm, dtype=jnp.int32) * tm
    tile_valid = (tile_start < gend[-1]).astype(jnp.int32)
    last_start = jnp.maximum(gend[-1] - tm, 0)
    probe = jnp.minimum(tile_start, last_start)
    tile_expert = jnp.minimum(jnp.sum((probe[:, None] >= gend[None, :]).astype(jnp.int32), axis=1),
                              N_EXPERTS - 1).astype(jnp.int32)
    dest2 = dest.reshape(n_tok, 2).astype(jnp.int32)
    return row_token, tile_expert, tile_valid, dest2[:, 0], dest2[:, 1]


def kernel(x, ev_norm1, ev_w_in, ev_lru_conv_w, ev_lru_conv_b, ev_lru_w_a, ev_lru_b_a, ev_lru_w_x, ev_lru_b_x, ev_lru_lambda, ev_w_out, ev_norm2, ev_ffn_w1, ev_ffn_w3, ev_ffn_w2, od_norm1, od_w_in, od_gla_w_a2, od_gla_b_a, od_gla_norm, od_ssd_conv_w, od_ssd_conv_b, od_ssd_dt_bias, od_ssd_a_log, od_ssd_d, od_ssd_norm, od_w_out, od_norm2, od_router, od_moe_w1, od_moe_w3, od_moe_w2, final_norm):
    bsz, t, d = x.shape
    n_tok = bsz * t
    xf = x.reshape(n_tok, d)

    lru_w = ev_lru_lambda.shape[1]
    moba_w = (ev_w_in.shape[2] - 2 * lru_w) // 3
    moba_dh = LANES
    h = rmsnorm(xf, ev_norm1[0])
    qkv = matmul([h], [ev_w_in[0]], w_col0=0, n_cols=3 * moba_w, out_dtype=BF16, name="l0_in_qkv")
    xg = matmul([h], [ev_w_in[0]], w_col0=3 * moba_w, n_cols=2 * lru_w, out_dtype=F32, name="l0_in_lru")
    att = moba_attention(qkv.reshape(bsz, t, 3 * moba_w), nh=moba_w // moba_dh, dh=moba_dh)
    rec = conv_rglru(xg.reshape(bsz, t, 2 * lru_w), ev_lru_conv_w[0], ev_lru_conv_b[0], ev_lru_w_a[0],
                     ev_lru_b_a[0], ev_lru_w_x[0], ev_lru_b_x[0], ev_lru_lambda[0])
    xf = matmul([att.reshape(n_tok, moba_w), rec.reshape(n_tok, lru_w)], [ev_w_out[0]], res=xf,
                name="l0_out")
    h = rmsnorm(xf, ev_norm2[0])
    gact = matmul([h], [ev_ffn_w1[0], ev_ffn_w3[0]], out_dtype=BF16, tm=1024, tn=256, name="l0_ffn_up")
    xf = matmul_ktiled_res(gact, ev_ffn_w2[0], xf, name="l0_ffn_down")

    gla_key = od_gla_w_a2.shape[2]
    gla_rank = od_gla_w_a2.shape[1]
    gla_dv = od_gla_norm.shape[1]
    gla_heads = gla_key // (gla_dv // 2)
    gla_val = gla_heads * gla_dv
    ssd_heads = od_ssd_a_log.shape[1]
    ssd_width = od_ssd_norm.shape[1]
    ssd_conv_dim = od_ssd_conv_w.shape[2]
    ssd_groups = 4
    ssd_state = (ssd_conv_dim - ssd_width) // (2 * ssd_groups)
    hpg = ssd_heads // ssd_groups
    w_in1 = od_w_in[0]
    c_lr = 2 * gla_key + 2 * gla_val
    c_z = c_lr + gla_rank
    c_dt = c_z + ssd_width + ssd_conv_dim
    zpad = lambda n: jnp.zeros((d, n), F32)
    w_small = jnp.concatenate(
        [w_in1[:, c_lr:c_z], zpad(LANES - gla_rank)]
        + [blk for g in range(ssd_groups)
           for blk in (w_in1[:, c_dt + g * hpg:c_dt + (g + 1) * hpg], zpad(LANES - hpg))], axis=1)
    w_zxbc = w_in1[:, c_z:c_dt]

    h = rmsnorm(xf, od_norm1[0])
    p1a = matmul([h], [w_in1], w_col0=0, n_cols=c_lr, out_dtype=BF16, name="l1_in_gla")
    p1b = matmul([h], [w_zxbc], out_dtype=F32, name="l1_in_ssd")
    sm = matmul([h], [w_small], out_dtype=F32, tn=w_small.shape[1], name="l1_in_small")
    w_a2p = jnp.pad(od_gla_w_a2[0], ((0, LANES - gla_rank), (0, 0)))
    o_gla = gla(p1a.reshape(bsz, t, c_lr), sm.reshape(bsz, t, -1), w_a2p, od_gla_b_a[0], od_gla_norm[0],
                nh=gla_heads, dk=gla_key // gla_heads, dv=gla_dv)
    y_ssd = ssd(p1b.reshape(bsz, t, -1), sm.reshape(bsz, t, -1), od_ssd_conv_w[0], od_ssd_conv_b[0],
                od_ssd_dt_bias[0], od_ssd_a_log[0], od_ssd_d[0], od_ssd_norm[0],
                nheads=ssd_heads, hd=ssd_width // ssd_heads, groups=ssd_groups, nstate=ssd_state)
    xf = matmul([o_gla.reshape(n_tok, gla_val), y_ssd.reshape(n_tok, ssd_width)], [od_w_out[0]], res=xf,
                name="l1_out")

    hf, route = norm_router(xf, od_norm2[0], od_router[0])
    ids = route[:, 0:2].astype(jnp.int32)
    row_token, tile_expert, tile_valid, dest0, dest1 = _moe_plan(ids, MOE_ROW_TILE)
    xs = moe_dispatch(row_token, hf)
    gact = moe_matmul(tile_expert, tile_valid, xs, [od_moe_w1[0], od_moe_w3[0]], out_dtype=BF16,
                      name="moe_up")
    ye = moe_matmul(tile_expert, tile_valid, gact, [od_moe_w2[0]], out_dtype=F32, name="moe_down")
    out = moe_combine_norm(dest0, dest1, xf, route, ye, final_norm)
    return out.reshape(bsz, t, d)
```

```python
import functools
import math

import jax
import jax.numpy as jnp
from jax import lax
from jax.experimental import pallas as pl
from jax.experimental.pallas import tpu as pltpu

F32 = jnp.float32
BF16 = jnp.bfloat16
HIGHEST = lax.Precision.HIGHEST

NORM_EPS = 1e-6
NEG = -1e30
LANES = 128
VMEM_LIMIT = 56 * 1024 * 1024

CONV_WIDTH = 4
CONV_PAD = 8
MOBA_BLOCK = 256
MOBA_TOPK = 3
LRU_C = 8.0
GLA_TAU = 16.0
GLA_SUB = 16
N_EXPERTS = 8
MOE_ROW_TILE = 256


def _nt(a, b, precision=None):
    return lax.dot_general(a, b, (((1,), (1,)), ((), ())), precision=precision,
                           preferred_element_type=F32)


def _dot_exact01(ones, x, *, ones_on_left):
    hi = x.astype(BF16)
    r1 = x - hi.astype(F32)
    mid = r1.astype(BF16)
    lo = (r1 - mid.astype(F32)).astype(BF16)
    out = None
    for term in (hi, mid, lo):
        part = (jnp.dot(ones, term, preferred_element_type=F32) if ones_on_left
                else jnp.dot(term, ones, preferred_element_type=F32))
        out = part if out is None else out + part
    return out


def _sigmoid(x):
    return 1.0 / (1.0 + jnp.exp(-x))


def _silu(x):
    return x * _sigmoid(x)


def _softplus(x):
    return jnp.maximum(x, 0.0) + jnp.log(1.0 + jnp.exp(-jnp.abs(x)))


def _params(*sem):
    return pltpu.CompilerParams(dimension_semantics=sem, vmem_limit_bytes=VMEM_LIMIT)


def _rmsnorm_kernel(x_ref, g_ref, o_ref):
    x = x_ref[...]
    y = x * lax.rsqrt(jnp.mean(x * x, axis=-1, keepdims=True) + NORM_EPS)
    o_ref[...] = (y * g_ref[...]).astype(o_ref.dtype)


def rmsnorm(x, g, out_dtype=BF16, tm=256):
    m, d = x.shape
    return pl.pallas_call(
        _rmsnorm_kernel,
        grid=(m // tm,),
        in_specs=[pl.BlockSpec((tm, d), lambda i: (i, 0)),
                  pl.BlockSpec((1, d), lambda i: (0, 0))],
        out_specs=pl.BlockSpec((tm, d), lambda i: (i, 0)),
        out_shape=jax.ShapeDtypeStruct((m, d), out_dtype),
        compiler_params=_params("parallel"),
        name="rmsnorm",
    )(x, g.reshape(1, d))


def _mm_kernel(*refs, n_a, has_res, act_pair):
    a_refs = refs[:n_a]
    n_w = 2 if act_pair else 1
    w_refs = refs[n_a:n_a + n_w]
    pos = n_a + n_w
    r_ref = refs[pos] if has_res else None
    pos += int(has_res)
    o_ref = refs[pos]
    wb_refs = refs[pos + 1:pos + 1 + n_w]

    @pl.when(pl.program_id(1) == 0)
    def _():
        for w_ref, wb_ref in zip(w_refs, wb_refs):
            wb_ref[...] = w_ref[...].astype(BF16)

    def contract(wb_ref):
        acc = None
        k0 = 0
        for a_ref in a_refs:
            ka = a_ref.shape[1]
            part = jnp.dot(a_ref[...], wb_ref[k0:k0 + ka, :], preferred_element_type=F32)
            acc = part if acc is None else acc + part
            k0 += ka
        return acc

    if act_pair:
        out = _silu(contract(wb_refs[0])) * contract(wb_refs[1])
    else:
        out = contract(wb_refs[0])
    if has_res:
        out = out + r_ref[...]
    o_ref[...] = out.astype(o_ref.dtype)


def matmul(a_list, w_list, *, w_col0=0, n_cols=None, res=None, out_dtype=F32, tm=1024, tn=512,
           name="matmul"):
    m = a_list[0].shape[0]
    k = sum(a.shape[1] for a in a_list)
    assert all(w.shape[0] == k for w in w_list)
    n_cols = w_list[0].shape[1] - w_col0 if n_cols is None else n_cols
    assert m % tm == 0 and n_cols % tn == 0 and w_col0 % tn == 0
    nb0 = w_col0 // tn
    in_specs = [pl.BlockSpec((tm, a.shape[1]), lambda n, i: (i, 0)) for a in a_list]
    in_specs += [pl.BlockSpec((k, tn), lambda n, i: (0, n + nb0)) for _ in w_list]
    args = list(a_list) + list(w_list)
    if res is not None:
        in_specs.append(pl.BlockSpec((tm, tn), lambda n, i: (i, n)))
        args.append(res)
    kern = functools.partial(_mm_kernel, n_a=len(a_list), has_res=res is not None,
                             act_pair=len(w_list) == 2)
    return pl.pallas_call(
        kern,
        grid=(n_cols // tn, m // tm),
        in_specs=in_specs,
        out_specs=pl.BlockSpec((tm, tn), lambda n, i: (i, n)),
        out_shape=jax.ShapeDtypeStruct((m, n_cols), out_dtype),
        scratch_shapes=[pltpu.VMEM((k, tn), BF16) for _ in w_list],
        compiler_params=_params("parallel", "arbitrary"),
        name=name,
    )(*args)


def _mm_ktiled_kernel(a_ref, w_ref, r_ref, o_ref):
    @pl.when(pl.program_id(2) == 0)
    def _():
        o_ref[...] = r_ref[...]

    o_ref[...] += jnp.dot(a_ref[...], w_ref[...].astype(BF16), preferred_element_type=F32)


def matmul_ktiled_res(a, w, res, *, tm=1024, tn=1024, tk=2048, name="matmul_ktiled"):
    m, k = a.shape
    n = w.shape[1]
    assert m % tm == 0 and n % tn == 0 and k % tk == 0
    return pl.pallas_call(
        _mm_ktiled_kernel,
        grid=(m // tm, n // tn, k // tk),
        in_specs=[pl.BlockSpec((tm, tk), lambda i, j, kk: (i, kk)),
                  pl.BlockSpec((tk, tn), lambda i, j, kk: (kk, j)),
                  pl.BlockSpec((tm, tn), lambda i, j, kk: (i, j))],
        out_specs=pl.BlockSpec((tm, tn), lambda i, j, kk: (i, j)),
        out_shape=jax.ShapeDtypeStruct((m, n), F32),
        compiler_params=_params("parallel", "parallel", "arbitrary"),
        name=name,
    )(a, w, res)


def _moba_kernel(q_ref, k_ref, v_ref, o_ref, kmean_ref, sel_ref, *, blk, nblk, topk, scale, span):
    qb = pl.program_id(2)
    nbp = kmean_ref.shape[0]

    @pl.when(qb == 0)
    def _():
        kmean_ref[...] = jnp.zeros_like(kmean_ref)
        for n in range(nblk):
            kb = k_ref[n * blk:(n + 1) * blk, :].astype(F32)
            kmean_ref[n:n + 1, :] = jnp.mean(kb, axis=0, keepdims=True)

    qs = q_ref[...].astype(F32) * scale
    gate = _nt(kmean_ref[...], qs, precision=HIGHEST)
    bid = lax.broadcasted_iota(jnp.int32, gate.shape, 0)
    t = jnp.where(bid < qb, gate, -jnp.inf)
    sel = jnp.zeros(gate.shape, F32)
    for _ in range(topk):
        mx = jnp.max(t, axis=0, keepdims=True)
        idx = jnp.min(jnp.where(t == mx, bid, nbp), axis=0, keepdims=True)
        pick = (bid == idx) & (mx > -jnp.inf)
        sel = jnp.where(pick, 1.0, sel)
        t = jnp.where(bid == idx, -jnp.inf, t)
    sel_ref[...] = jnp.concatenate([sel, jnp.zeros((LANES - nbp, blk), F32)], axis=0).T

    qsb = qs.astype(BF16)
    row = lax.broadcasted_iota(jnp.int32, (blk, blk), 0)
    col = lax.broadcasted_iota(jnp.int32, (blk, blk), 1)
    ahead = col - row
    for nv in range(span, nblk + 1, span):
        @pl.when((qb >= nv - span) & (qb < nv))
        def _(nv=nv):
            s = _nt(qsb, k_ref[0:nv * blk, :])
            pieces = []
            for kb in range(nv):
                sb = s[:, kb * blk:(kb + 1) * blk]
                if kb >= nv - span:
                    own_limit = jnp.where(qb == kb, 0, -2 * blk)
                    piece = jnp.where(ahead <= own_limit, sb, NEG)
                else:
                    piece = NEG
                pieces.append(jnp.where(sel_ref[:, kb:kb + 1] > 0.5, sb, piece))
            sm = jnp.concatenate(pieces, axis=1)
            m = jnp.max(sm, axis=-1, keepdims=True)
            p = jnp.exp(sm - m)
            l = jnp.sum(p, axis=-1, keepdims=True)
            o = jnp.dot(p.astype(BF16), v_ref[0:nv * blk, :], preferred_element_type=F32)
            o_ref[...] = (o / l).astype(o_ref.dtype)


def moba_attention(qkv, *, nh, dh, span=2):
    bsz, t, _ = qkv.shape
    blk = MOBA_BLOCK
    nblk = t // blk
    nbp = -(-nblk // 8) * 8
    assert t % blk == 0 and nbp <= LANES and dh % LANES == 0 and nblk % span == 0
    kern = functools.partial(_moba_kernel, blk=blk, nblk=nblk, topk=min(MOBA_TOPK, nblk),
                             scale=dh ** -0.5, span=span)
    return pl.pallas_call(
        kern,
        grid=(bsz, nh, nblk),
        in_specs=[pl.BlockSpec((None, blk, dh), lambda b, h, i: (b, i, h)),
                  pl.BlockSpec((None, t, dh), lambda b, h, i: (b, 0, nh + h)),
                  pl.BlockSpec((None, t, dh), lambda b, h, i: (b, 0, 2 * nh + h))],
        out_specs=pl.BlockSpec((None, blk, dh), lambda b, h, i: (b, i, h)),
        out_shape=jax.ShapeDtypeStruct((bsz, t, nh * dh), BF16),
        scratch_shapes=[pltpu.VMEM((nbp, dh), F32),
                        pltpu.VMEM((blk, LANES), F32)],
        compiler_params=_params("parallel", "parallel", "arbitrary"),
        name="moba_attention",
    )(qkv, qkv, qkv)


def _causal_conv(x, buf_ref, w_ref, b_ref, first):
    rows = x.shape[0]

    @pl.when(first)
    def _():
        buf_ref[0:CONV_PAD, :] = jnp.zeros((CONV_PAD, x.shape[1]), F32)

    buf_ref[CONV_PAD:CONV_PAD + rows, :] = x
    y = b_ref[...] + w_ref[CONV_WIDTH - 1:CONV_WIDTH, :] * x
    for kk in range(CONV_WIDTH - 1):
        off = CONV_PAD - (CONV_WIDTH - 1) + kk
        y = y + w_ref[kk:kk + 1, :] * buf_ref[off:off + rows, :]
    buf_ref[0:CONV_PAD, :] = x[rows - CONV_PAD:rows, :]
    return y


def _lru_kernel(xr_ref, gr_ref, cw_ref, cb_ref, wa_ref, ba_ref, wx_ref, bx_ref, lam_ref, o_ref,
                xbuf_ref, h_ref, *, nblocks, bdim):
    ti = pl.program_id(2)
    rows = xr_ref.shape[0]

    @pl.when(ti == 0)
    def _():
        h_ref[...] = jnp.zeros_like(h_ref)

    x = _causal_conv(xr_ref[...], xbuf_ref, cw_ref, cb_ref, ti == 0)
    ga, gx = [], []
    for n in range(nblocks):
        xb = x[:, n * bdim:(n + 1) * bdim].astype(BF16)
        ga.append(jnp.dot(xb, wa_ref[n].astype(BF16), preferred_element_type=F32))
        gx.append(jnp.dot(xb, wx_ref[n].astype(BF16), preferred_element_type=F32))
    gate_a = jnp.concatenate(ga, axis=-1) + ba_ref[...]
    gate_x = jnp.concatenate(gx, axis=-1) + bx_ref[...]
    log_a = -LRU_C * _sigmoid(gate_a) * _softplus(-lam_ref[...])
    a = jnp.exp(log_a)
    u = jnp.sqrt(1.0 - jnp.exp(2.0 * log_a)) * _sigmoid(gate_x) * x

    rid = lax.broadcasted_iota(jnp.int32, a.shape, 0)
    s = 1
    while s < rows:
        keep = rid >= s
        a_prev = jnp.where(keep, pltpu.roll(a, s, 0), 1.0)
        u_prev = jnp.where(keep, pltpu.roll(u, s, 0), 0.0)
        u = a * u_prev + u
        a = a * a_prev
        s *= 2
    h = u + a * h_ref[0:1, :]
    h_ref[0:1, :] = h[rows - 1:rows, :]

    g = gr_ref[...]
    gelu = 0.5 * g * (1.0 + jnp.tanh(math.sqrt(2.0 / math.pi) * (g + 0.044715 * g * g * g)))
    o_ref[...] = (h * gelu).astype(o_ref.dtype)


def conv_rglru(xg, conv_w, conv_b, w_a, b_a, w_x, b_x, lam, *, tt=256, tc=512):
    bsz, t, w2 = xg.shape
    w = w2 // 2
    nblocks_all, bdim, _ = w_a.shape
    nb = tc // bdim
    assert t % tt == 0 and w % tc == 0 and tc % bdim == 0
    row = lambda v: v.reshape(1, w)
    vec_spec = pl.BlockSpec((1, tc), lambda b, c, i: (0, c))
    gate_spec = pl.BlockSpec((nb, bdim, bdim), lambda b, c, i: (c, 0, 0))
    kern = functools.partial(_lru_kernel, nblocks=nb, bdim=bdim)
    return pl.pallas_call(
        kern,
        grid=(bsz, w // tc, t // tt),
        in_specs=[pl.BlockSpec((None, tt, tc), lambda b, c, i: (b, i, c)),
                  pl.BlockSpec((None, tt, tc), lambda b, c, i: (b, i, w // tc + c)),
                  pl.BlockSpec((CONV_WIDTH, tc), lambda b, c, i: (0, c)),
                  vec_spec, gate_spec, vec_spec, gate_spec, vec_spec, vec_spec],
        out_specs=pl.BlockSpec((None, tt, tc), lambda b, c, i: (b, i, c)),
        out_shape=jax.ShapeDtypeStruct((bsz, t, w), BF16),
        scratch_shapes=[pltpu.VMEM((CONV_PAD + tt, tc), F32),
                        pltpu.VMEM((8, tc), F32)],
        compiler_params=_params("parallel", "parallel", "arbitrary"),
        name="conv_rglru",
    )(xg, xg, conv_w, row(conv_b), w_a, row(b_a), w_x, row(b_x), row(lam))


def _gla_kernel(q_ref, k_ref, v_ref, r_ref, lr_ref, wa_ref, ba_ref, g_ref, o_ref, st_ref, *,
                chunk, scale):
    ci = pl.program_id(2)

    @pl.when(ci == 0)
    def _():
        st_ref[...] = jnp.zeros_like(st_ref)

    dk = q_ref.shape[1]
    z = jnp.dot(lr_ref[...].astype(BF16), wa_ref[...].astype(BF16),
                preferred_element_type=F32) + ba_ref[...]
    g = (jnp.minimum(z, 0.0) - jnp.log(1.0 + jnp.exp(-jnp.abs(z)))) / GLA_TAU
    row = lax.broadcasted_iota(jnp.int32, (chunk, chunk), 0)
    col = lax.broadcasted_iota(jnp.int32, (chunk, chunk), 1)
    sub0 = row - jnp.bitwise_and(row, GLA_SUB - 1)
    sums = jnp.concatenate([jnp.where(col <= row, 1.0, 0.0), jnp.where(col < sub0, 1.0, 0.0)],
                           axis=0).astype(BF16)
    bb = _dot_exact01(sums, g, ones_on_left=True)
    b = bb[0:chunk]
    b_sub = bb[chunk:2 * chunk]
    q = q_ref[...].astype(F32) * scale
    k = k_ref[...].astype(F32)
    v = v_ref[...]
    b_last = b[chunk - 1:chunk, :]

    o = _nt((q * jnp.exp(b)).astype(BF16), st_ref[...].astype(BF16))

    jrow = lax.broadcasted_iota(jnp.int32, (chunk, dk), 0)
    srow = lax.broadcasted_iota(jnp.int32, (GLA_SUB, chunk), 0)
    scol = lax.broadcasted_iota(jnp.int32, (GLA_SUB, chunk), 1)
    parts = []
    for i in range(chunk // GLA_SUB):
        lo, hi = i * GLA_SUB, (i + 1) * GLA_SUB
        ref = b_sub[lo:lo + 1, :]
        qi = (q[lo:hi] * jnp.exp(b[lo:hi] - ref)).astype(BF16)
        ki = (k * jnp.exp(jnp.where(jrow < hi, ref - b, 0.0))).astype(BF16)
        att = _nt(qi, ki)
        att = jnp.where(scol <= srow + lo, att, 0.0)
        parts.append(jnp.dot(att.astype(BF16), v, preferred_element_type=F32))
    o = o + jnp.concatenate(parts, axis=0)

    kd = (k * jnp.exp(b_last - b)).astype(BF16)
    upd = lax.dot_general(v, kd, (((0,), (0,)), ((), ())), preferred_element_type=F32)
    st_ref[...] = st_ref[...] * jnp.exp(b_last) + upd

    y = o * lax.rsqrt(jnp.mean(o * o, axis=-1, keepdims=True) + NORM_EPS) * g_ref[...]
    o_ref[...] = (y * _silu(r_ref[...].astype(F32))).astype(o_ref.dtype)


def gla(p1a, sm, w_a2p, b_a, norm, *, nh, dk, dv, chunk=128):
    bsz, t, _ = p1a.shape
    assert t % chunk == 0 and chunk % GLA_SUB == 0
    kq = nh * dk
    voff = 2 * kq // dv
    roff = voff + nh
    kern = functools.partial(_gla_kernel, chunk=chunk, scale=dk ** -0.5)
    return pl.pallas_call(
        kern,
        grid=(bsz, nh, t // chunk),
        in_specs=[pl.BlockSpec((None, chunk, dk), lambda b, h, c: (b, c, h)),
                  pl.BlockSpec((None, chunk, dk), lambda b, h, c: (b, c, nh + h)),
                  pl.BlockSpec((None, chunk, dv), lambda b, h, c: (b, c, voff + h)),
                  pl.BlockSpec((None, chunk, dv), lambda b, h, c: (b, c, roff + h)),
                  pl.BlockSpec((None, chunk, LANES), lambda b, h, c: (b, c, 0)),
                  pl.BlockSpec((LANES, dk), lambda b, h, c: (0, h)),
                  pl.BlockSpec((1, dk), lambda b, h, c: (0, h)),
                  pl.BlockSpec((1, dv), lambda b, h, c: (0, 0))],
        out_specs=pl.BlockSpec((None, chunk, dv), lambda b, h, c: (b, c, h)),
        out_shape=jax.ShapeDtypeStruct((bsz, t, nh * dv), BF16),
        scratch_shapes=[pltpu.VMEM((dv, dk), F32)],
        compiler_params=_params("parallel", "parallel", "arbitrary"),
        name="gla",
    )(p1a, p1a, p1a, p1a, sm, w_a2p, b_a.reshape(1, kq), norm.reshape(1, dv))


def _ssd_kernel(z_ref, xs_ref, bm_ref, cm_ref, dt_ref, cwx_ref, cwb_ref, cwc_ref, cbx_ref, cbb_ref,
                cbc_ref, dtb_ref, alog_ref, dfull_ref, norm_ref, o_ref,
                xbuf_ref, bbuf_ref, cbuf_ref, st_ref, *, chunk, hpg, hd):
    ci = pl.program_id(2)
    first = ci == 0

    @pl.when(first)
    def _():
        st_ref[...] = jnp.zeros_like(st_ref)

    xs = _silu(_causal_conv(xs_ref[...], xbuf_ref, cwx_ref, cbx_ref, first))
    bm = _silu(_causal_conv(bm_ref[...], bbuf_ref, cwb_ref, cbb_ref, first))
    cm = _silu(_causal_conv(cm_ref[...], cbuf_ref, cwc_ref, cbc_ref, first))
    width = hpg * hd

    dt = _softplus(dt_ref[...] + dtb_ref[...])
    da = dt * (-jnp.exp(alog_ref[...]))
    row = lax.broadcasted_iota(jnp.int32, (chunk, chunk), 0)
    col = lax.broadcasted_iota(jnp.int32, (chunk, chunk), 1)
    causal = col <= row
    tril = jnp.where(causal, 1.0, 0.0).astype(BF16)
    cs = _dot_exact01(tril, da, ones_on_left=True)
    cs_t = cs.T

    erow = lax.broadcasted_iota(jnp.int32, (LANES, width), 0)
    ecol = lax.broadcasted_iota(jnp.int32, (LANES, width), 1)
    expand = jnp.where((ecol >= erow * hd) & (ecol < (erow + 1) * hd), 1.0, 0.0).astype(BF16)
    full = _dot_exact01(expand, jnp.concatenate([cs, dt], axis=0), ones_on_left=False)
    cs_full = full[0:chunk]
    dt_full = full[chunk:2 * chunk]
    cs_last = cs_full[chunk - 1:chunk, :]

    xd = xs * dt_full
    xd_b = xd.astype(BF16)
    bm_b = bm.astype(BF16)
    cm_b = cm.astype(BF16)
    cb = _nt(cm_b, bm_b)

    lane = lax.broadcasted_iota(jnp.int32, (chunk, 2 * hd), 1)
    slabs = []
    for j in range(hpg // 2):
        xpair = xd_b[:, 2 * j * hd:(2 * j + 2) * hd]
        acc = None
        for half in range(2):
            h = 2 * j + half
            seg = cs[:, h:h + 1] - cs_t[h:h + 1, :]
            lmat = jnp.exp(jnp.where(causal, seg, NEG))
            mh = (cb * lmat).astype(BF16)
            mine = (lane >= half * hd) & (lane < (half + 1) * hd)
            part = jnp.dot(mh, jnp.where(mine, xpair, jnp.zeros_like(xpair)),
                           preferred_element_type=F32)
            acc = part if acc is None else acc + part
        slabs.append(acc)
    y = jnp.concatenate(slabs, axis=-1)

    st = st_ref[...]
    y = y + jnp.dot(cm_b, st.astype(BF16), preferred_element_type=F32) * jnp.exp(cs_full)
    xdd = (xd * jnp.exp(cs_last - cs_full)).astype(BF16)
    upd = lax.dot_general(bm_b, xdd, (((0,), (0,)), ((), ())), preferred_element_type=F32)
    st_ref[...] = st * jnp.exp(cs_last) + upd

    y = y + dfull_ref[...] * xs
    y = y * _silu(z_ref[...])
    y = y * lax.rsqrt(jnp.mean(y * y, axis=-1, keepdims=True) + NORM_EPS) * norm_ref[...]
    o_ref[...] = y.astype(o_ref.dtype)


def ssd(p1b, sm, conv_w, conv_b, dt_bias, a_log, d_skip, norm, *, nheads, hd, groups, nstate,
        chunk=256):
    bsz, t, _ = p1b.shape
    width = nheads * hd
    gw = width // groups
    hpg = nheads // groups
    assert t % chunk == 0 and nstate == LANES and gw % LANES == 0 and hpg % 2 == 0
    xoff = width // gw
    boff = 2 * width // nstate
    coff = boff + groups
    cxo, cbo, cco = 0, width // nstate, width // nstate + groups
    padl = lambda v: jnp.pad(v.reshape(groups, 1, hpg), ((0, 0), (0, 0), (0, LANES - hpg)))
    dfull = jnp.repeat(d_skip.reshape(groups, hpg), hd, axis=1).reshape(groups, 1, gw)
    cb2 = conv_b.reshape(1, -1)
    g3 = lambda b, g, c: (g, 0, 0)
    kern = functools.partial(_ssd_kernel, chunk=chunk, hpg=hpg, hd=hd)
    return pl.pallas_call(
        kern,
        grid=(bsz, groups, t // chunk),
        in_specs=[pl.BlockSpec((None, chunk, gw), lambda b, g, c: (b, c, g)),
                  pl.BlockSpec((None, chunk, gw), lambda b, g, c: (b, c, xoff + g)),
                  pl.BlockSpec((None, chunk, nstate), lambda b, g, c: (b, c, boff + g)),
                  pl.BlockSpec((None, chunk, nstate), lambda b, g, c: (b, c, coff + g)),
                  pl.BlockSpec((None, chunk, LANES), lambda b, g, c: (b, c, 1 + g)),
                  pl.BlockSpec((CONV_WIDTH, gw), lambda b, g, c: (0, cxo + g)),
                  pl.BlockSpec((CONV_WIDTH, nstate), lambda b, g, c: (0, cbo + g)),
                  pl.BlockSpec((CONV_WIDTH, nstate), lambda b, g, c: (0, cco + g)),
                  pl.BlockSpec((1, gw), lambda b, g, c: (0, cxo + g)),
                  pl.BlockSpec((1, nstate), lambda b, g, c: (0, cbo + g)),
                  pl.BlockSpec((1, nstate), lambda b, g, c: (0, cco + g)),
                  pl.BlockSpec((None, 1, LANES), g3),
                  pl.BlockSpec((None, 1, LANES), g3),
                  pl.BlockSpec((None, 1, gw), g3),
                  pl.BlockSpec((1, gw), lambda b, g, c: (0, g))],
        out_specs=pl.BlockSpec((None, chunk, gw), lambda b, g, c: (b, c, g)),
        out_shape=jax.ShapeDtypeStruct((bsz, t, width), BF16),
        scratch_shapes=[pltpu.VMEM((CONV_PAD + chunk, gw), F32),
                        pltpu.VMEM((CONV_PAD + chunk, nstate), F32),
                        pltpu.VMEM((CONV_PAD + chunk, nstate), F32),
                        pltpu.VMEM((nstate, gw), F32)],
        compiler_params=_params("parallel", "parallel", "arbitrary"),
        name="ssd",
    )(p1b, p1b, p1b, p1b, sm, conv_w, conv_w, conv_w, cb2, cb2, cb2,
      padl(dt_bias), padl(a_log), dfull, norm.reshape(1, width))


def _norm_router_kernel(x_ref, g_ref, r_ref, h_ref, route_ref, *, n_exp):
    x = x_ref[...]
    y = x * lax.rsqrt(jnp.mean(x * x, axis=-1, keepdims=True) + NORM_EPS) * g_ref[...]
    h_ref[...] = y
    logits = jnp.dot(y, r_ref[...], precision=HIGHEST, preferred_element_type=F32)
    lane = lax.broadcasted_iota(jnp.int32, logits.shape, 1)
    t = jnp.where(lane < n_exp, logits, -jnp.inf)
    m1 = jnp.max(t, axis=-1, keepdims=True)
    i1 = jnp.min(jnp.where(t == m1, lane, LANES), axis=-1, keepdims=True)
    t2 = jnp.where(lane == i1, -jnp.inf, t)
    m2 = jnp.max(t2, axis=-1, keepdims=True)
    i2 = jnp.min(jnp.where(t2 == m2, lane, LANES), axis=-1, keepdims=True)
    e = jnp.exp(m2 - m1)
    w1 = 1.0 / (1.0 + e)
    w2 = e / (1.0 + e)
    route = jnp.where(lane == 0, i1.astype(F32),
                      jnp.where(lane == 1, i2.astype(F32),
                                jnp.where(lane == 2, w1, jnp.where(lane == 3, w2, 0.0))))
    route_ref[...] = route


def norm_router(x, g, router, tm=256):
    m, d = x.shape
    n_exp = router.shape[1]
    rp = jnp.pad(router, ((0, 0), (0, LANES - n_exp)))
    return pl.pallas_call(
        functools.partial(_norm_router_kernel, n_exp=n_exp),
        grid=(m // tm,),
        in_specs=[pl.BlockSpec((tm, d), lambda i: (i, 0)),
                  pl.BlockSpec((1, d), lambda i: (0, 0)),
                  pl.BlockSpec((d, LANES), lambda i: (0, 0))],
        out_specs=[pl.BlockSpec((tm, d), lambda i: (i, 0)),
                   pl.BlockSpec((tm, LANES), lambda i: (i, 0))],
        out_shape=[jax.ShapeDtypeStruct((m, d), F32), jax.ShapeDtypeStruct((m, LANES), F32)],
        compiler_params=_params("parallel"),
        name="norm_router",
    )(x, g.reshape(1, d), rp)


def _row_gather(idx_ref, base, src_hbm, dst_ref, sem, rows):
    def start(r, c):
        pltpu.make_async_copy(src_hbm.at[pl.ds(idx_ref[base + r], 1)],
                              dst_ref.at[pl.ds(r, 1)], sem).start()
        return c

    lax.fori_loop(0, rows, start, 0, unroll=8)
    pltpu.make_async_copy(src_hbm.at[pl.ds(0, rows)], dst_ref, sem).wait()


def _dispatch_kernel(tok_ref, h_hbm, o_ref, buf_ref, sem):
    rows = buf_ref.shape[0]
    _row_gather(tok_ref, pl.program_id(0) * rows, h_hbm, buf_ref, sem, rows)
    o_ref[...] = buf_ref[...].astype(o_ref.dtype)


def moe_dispatch(row_token, h, tm=MOE_ROW_TILE):
    r = row_token.shape[0]
    d = h.shape[1]
    return pl.pallas_call(
        _dispatch_kernel,
        grid_spec=pltpu.PrefetchScalarGridSpec(
            num_scalar_prefetch=1,
            grid=(r // tm,),
            in_specs=[pl.BlockSpec(memory_space=pl.ANY)],
            out_specs=pl.BlockSpec((tm, d), lambda i, tok: (i, 0)),
            scratch_shapes=[pltpu.VMEM((tm, d), F32), pltpu.SemaphoreType.DMA(())]),
        out_shape=jax.ShapeDtypeStruct((r, d), BF16),
        compiler_params=_params("arbitrary"),
        name="moe_dispatch",
    )(row_token, h)


def _moe_mm_kernel(te_ref, tv_ref, x_ref, *refs, act_pair):
    n_w = 2 if act_pair else 1
    w_refs = refs[:n_w]
    o_ref = refs[n_w]
    wb_refs = refs[n_w + 1:]
    r = pl.program_id(1)

    @pl.when((r == 0) | (te_ref[r] != te_ref[jnp.maximum(r - 1, 0)]))
    def _():
        for w_ref, wb_ref in zip(w_refs, wb_refs):
            wb_ref[...] = w_ref[...].astype(BF16)

    @pl.when(tv_ref[r] > 0)
    def _():
        x = x_ref[...]
        out = jnp.dot(x, wb_refs[0][...], preferred_element_type=F32)
        if act_pair:
            out = _silu(out) * jnp.dot(x, wb_refs[1][...], preferred_element_type=F32)
        o_ref[...] = out.astype(o_ref.dtype)

    @pl.when(tv_ref[r] == 0)
    def _():
        o_ref[...] = jnp.zeros_like(o_ref)


def moe_matmul(tile_expert, tile_valid, x, w_list, *, out_dtype, tm=MOE_ROW_TILE, tn=256,
               name="moe_matmul"):
    r, k = x.shape
    n = w_list[0].shape[2]
    assert r % tm == 0 and n % tn == 0
    kern = functools.partial(_moe_mm_kernel, act_pair=len(w_list) == 2)
    w_spec = pl.BlockSpec((None, k, tn), lambda j, i, te, tv: (te[i], 0, j))
    return pl.pallas_call(
        kern,
        grid_spec=pltpu.PrefetchScalarGridSpec(
            num_scalar_prefetch=2,
            grid=(n // tn, r // tm),
            in_specs=[pl.BlockSpec((tm, k), lambda j, i, te, tv: (i, 0))] + [w_spec] * len(w_list),
            out_specs=pl.BlockSpec((tm, tn), lambda j, i, te, tv: (i, j)),
            scratch_shapes=[pltpu.VMEM((k, tn), BF16) for _ in w_list]),
        out_shape=jax.ShapeDtypeStruct((r, n), out_dtype),
        compiler_params=_params("parallel", "arbitrary"),
        name=name,
    )(tile_expert, tile_valid, x, *w_list)


def _combine_kernel(d0_ref, d1_ref, x_ref, route_ref, g_ref, y_hbm, o_ref, buf0_ref, buf1_ref, sems):
    rows = x_ref.shape[0]
    base = pl.program_id(0) * rows
    _row_gather(d0_ref, base, y_hbm, buf0_ref, sems.at[0], rows)
    _row_gather(d1_ref, base, y_hbm, buf1_ref, sems.at[1], rows)
    w0 = route_ref[:, 2:3]
    w1 = route_ref[:, 3:4]
    x = x_ref[...] + w0 * buf0_ref[...] + w1 * buf1_ref[...]
    y = x * lax.rsqrt(jnp.mean(x * x, axis=-1, keepdims=True) + NORM_EPS)
    o_ref[...] = y * g_ref[...]


def moe_combine_norm(dest0, dest1, x, route, y, g, tm=256):
    m, d = x.shape
    return pl.pallas_call(
        _combine_kernel,
        grid_spec=pltpu.PrefetchScalarGridSpec(
            num_scalar_prefetch=2,
            grid=(m // tm,),
            in_specs=[pl.BlockSpec((tm, d), lambda i, a, b: (i, 0)),
                      pl.BlockSpec((tm, LANES), lambda i, a, b: (i, 0)),
                      pl.BlockSpec((1, d), lambda i, a, b: (0, 0)),
                      pl.BlockSpec(memory_space=pl.ANY)],
            out_specs=pl.BlockSpec((tm, d), lambda i, a, b: (i, 0)),
            scratch_shapes=[pltpu.VMEM((tm, d), F32), pltpu.VMEM((tm, d), F32),
                            pltpu.SemaphoreType.DMA((2,))]),
        out_shape=jax.ShapeDtypeStruct((m, d), F32),
        compiler_params=_params("arbitrary"),
        name="moe_combine_norm",
    )(dest0, dest1, x, route, g.reshape(1, d), y)


def _moe_plan(ids, tm):
    n_tok = ids.shape[0]
    e_flat = ids.reshape(-1)
    onehot = (e_flat[:, None] == jnp.arange(N_EXPERTS, dtype=jnp.int32)[None, :]).astype(jnp.int32)
    csum = jnp.cumsum(onehot, axis=0)
    pos = jnp.sum(csum * onehot, axis=1) - 1
    counts = csum[-1]
    padded = ((counts + tm - 1) // tm) * tm
    gend = jnp.cumsum(padded)
    gstart = gend - padded
    dest = gstart[e_flat] + pos
    n_rows = 2 * n_tok + N_EXPERTS * tm
    row_token = jnp.zeros((n_rows,), jnp.int32).at[dest].set(
        jnp.arange(2 * n_tok, dtype=jnp.int32) // 2)
    tile_start = jnp.arange(n_rows // tm, dtype=jnp.int32) * tm
    tile_valid = (tile_start < gend[-1]).astype(jnp.int32)
    last_start = jnp.maximum(gend[-1] - tm, 0)
    probe = jnp.minimum(tile_start, last_start)
    tile_expert = jnp.minimum(jnp.sum((probe[:, None] >= gend[None, :]).astype(jnp.int32), axis=1),
                              N_EXPERTS - 1).astype(jnp.int32)
    dest2 = dest.reshape(n_tok, 2).astype(jnp.int32)
    return row_token, tile_expert, tile_valid, dest2[:, 0], dest2[:, 1]


def kernel(x, ev_norm1, ev_w_in, ev_lru_conv_w, ev_lru_conv_b, ev_lru_w_a, ev_lru_b_a, ev_lru_w_x, ev_lru_b_x, ev_lru_lambda, ev_w_out, ev_norm2, ev_ffn_w1, ev_ffn_w3, ev_ffn_w2, od_norm1, od_w_in, od_gla_w_a2, od_gla_b_a, od_gla_norm, od_ssd_conv_w, od_ssd_conv_b, od_ssd_dt_bias, od_ssd_a_log, od_ssd_d, od_ssd_norm, od_w_out, od_norm2, od_router, od_moe_w1, od_moe_w3, od_moe_w2, final_norm):
    bsz, t, d = x.shape
    n_tok = bsz * t
    xf = x.reshape(n_tok, d)

    lru_w = ev_lru_lambda.shape[1]
    moba_w = (ev_w_in.shape[2] - 2 * lru_w) // 3
    moba_dh = LANES
    h = rmsnorm(xf, ev_norm1[0])
    qkv = matmul([h], [ev_w_in[0]], w_col0=0, n_cols=3 * moba_w, out_dtype=BF16, name="l0_in_qkv")
    xg = matmul([h], [ev_w_in[0]], w_col0=3 * moba_w, n_cols=2 * lru_w, out_dtype=F32, name="l0_in_lru")
    att = moba_attention(qkv.reshape(bsz, t, 3 * moba_w), nh=moba_w // moba_dh, dh=moba_dh)
    rec = conv_rglru(xg.reshape(bsz, t, 2 * lru_w), ev_lru_conv_w[0], ev_lru_conv_b[0], ev_lru_w_a[0],
                     ev_lru_b_a[0], ev_lru_w_x[0], ev_lru_b_x[0], ev_lru_lambda[0])
    xf = matmul([att.reshape(n_tok, moba_w), rec.reshape(n_tok, lru_w)], [ev_w_out[0]], res=xf,
                name="l0_out")
    h = rmsnorm(xf, ev_norm2[0])
    gact = matmul([h], [ev_ffn_w1[0], ev_ffn_w3[0]], out_dtype=BF16, tm=1024, tn=256, name="l0_ffn_up")
    xf = matmul_ktiled_res(gact, ev_ffn_w2[0], xf, name="l0_ffn_down")

    gla_key = od_gla_w_a2.shape[2]
    gla_rank = od_gla_w_a2.shape[1]
    gla_dv = od_gla_norm.shape[1]
    gla_heads = gla_key // (gla_dv // 2)
    gla_val = gla_heads * gla_dv
    ssd_heads = od_ssd_a_log.shape[1]
    ssd_width = od_ssd_norm.shape[1]
    ssd_conv_dim = od_ssd_conv_w.shape[2]
    ssd_groups = 4
    ssd_state = (ssd_conv_dim - ssd_width) // (2 * ssd_groups)
    hpg = ssd_heads // ssd_groups
    w_in1 = od_w_in[0]
    c_lr = 2 * gla_key + 2 * gla_val
    c_z = c_lr + gla_rank
    c_dt = c_z + ssd_width + ssd_conv_dim
    zpad = lambda n: jnp.zeros((d, n), F32)
    w_small = jnp.concatenate(
        [w_in1[:, c_lr:c_z], zpad(LANES - gla_rank)]
        + [blk for g in range(ssd_groups)
           for blk in (w_in1[:, c_dt + g * hpg:c_dt + (g + 1) * hpg], zpad(LANES - hpg))], axis=1)
    w_zxbc = w_in1[:, c_z:c_dt].astype(BF16)

    h = rmsnorm(xf, od_norm1[0])
    p1a = matmul([h], [w_in1], w_col0=0, n_cols=c_lr, out_dtype=BF16, name="l1_in_gla")
    p1b = matmul([h], [w_zxbc], out_dtype=F32, name="l1_in_ssd")
    sm = matmul([h], [w_small], out_dtype=F32, tn=w_small.shape[1], name="l1_in_small")
    w_a2p = jnp.pad(od_gla_w_a2[0], ((0, LANES - gla_rank), (0, 0)))
    o_gla = gla(p1a.reshape(bsz, t, c_lr), sm.reshape(bsz, t, -1), w_a2p, od_gla_b_a[0], od_gla_norm[0],
                nh=gla_heads, dk=gla_key // gla_heads, dv=gla_dv)
    y_ssd = ssd(p1b.reshape(bsz, t, -1), sm.reshape(bsz, t, -1), od_ssd_conv_w[0], od_ssd_conv_b[0],
                od_ssd_dt_bias[0], od_ssd_a_log[0], od_ssd_d[0], od_ssd_norm[0],
                nheads=ssd_heads, hd=ssd_width // ssd_heads, groups=ssd_groups, nstate=ssd_state)
    xf = matmul([o_gla.reshape(n_tok, gla_val), y_ssd.reshape(n_tok, ssd_width)], [od_w_out[0]], res=xf,
                name="l1_out")

    hf, route = norm_router(xf, od_norm2[0], od_router[0])
    ids = route[:, 0:2].astype(jnp.int32)
    row_token, tile_expert, tile_valid, dest0, dest1 = _moe_plan(ids, MOE_ROW_TILE)
    xs = moe_dispatch(row_token, hf)
    gact = moe_matmul(tile_expert, tile_valid, xs, [od_moe_w1[0], od_moe_w3[0]], out_dtype=BF16,
                      tn=512, name="moe_up")
    ye = moe_matmul(tile_expert, tile_valid, gact, [od_moe_w2[0]], out_dtype=F32, tn=1024,
                    name="moe_down")
    out = moe_combine_norm(dest0, dest1, xf, route, ye, final_norm)
    return out.reshape(bsz, t, d)
```

```python
import functools
import math

import jax
import jax.numpy as jnp
from jax import lax
from jax.experimental import pallas as pl
from jax.experimental.pallas import tpu as pltpu

F32 = jnp.float32
BF16 = jnp.bfloat16
HIGHEST = lax.Precision.HIGHEST

NORM_EPS = 1e-6
NEG = -1e30
LANES = 128
SUBLANES = 8
VMEM_LIMIT = 56 * 1024 * 1024

CONV_WIDTH = 4
CONV_PAD = 8
MOBA_BLOCK = 256
MOBA_TOPK = 3
LRU_C = 8.0
GLA_TAU = 16.0
GLA_SUB = 16
N_EXPERTS = 8
MOE_ROW_TILE = 512
MOE_COL_TILE = 512
GATHER_UNROLL = 8


def _nt(a, b, precision=None):
    return lax.dot_general(a, b, (((1,), (1,)), ((), ())), precision=precision,
                           preferred_element_type=F32)


def _dot_exact01(ones, x, *, ones_on_left):
    hi = x.astype(BF16)
    r1 = x - hi.astype(F32)
    mid = r1.astype(BF16)
    lo = (r1 - mid.astype(F32)).astype(BF16)
    out = None
    for term in (hi, mid, lo):
        part = (jnp.dot(ones, term, preferred_element_type=F32) if ones_on_left
                else jnp.dot(term, ones, preferred_element_type=F32))
        out = part if out is None else out + part
    return out


def _sigmoid(x):
    return 1.0 / (1.0 + jnp.exp(-x))


def _silu(x):
    return x * _sigmoid(x)


def _softplus(x):
    return jnp.maximum(x, 0.0) + jnp.log(1.0 + jnp.exp(-jnp.abs(x)))


def _params(*sem):
    return pltpu.CompilerParams(dimension_semantics=sem, vmem_limit_bytes=VMEM_LIMIT)


def _rmsnorm_kernel(x_ref, g_ref, o_ref):
    x = x_ref[...]
    y = x * lax.rsqrt(jnp.mean(x * x, axis=-1, keepdims=True) + NORM_EPS)
    o_ref[...] = (y * g_ref[...]).astype(o_ref.dtype)


def rmsnorm(x, g, out_dtype=BF16, tm=256):
    m, d = x.shape
    return pl.pallas_call(
        _rmsnorm_kernel,
        grid=(m // tm,),
        in_specs=[pl.BlockSpec((tm, d), lambda i: (i, 0)),
                  pl.BlockSpec((1, d), lambda i: (0, 0))],
        out_specs=pl.BlockSpec((tm, d), lambda i: (i, 0)),
        out_shape=jax.ShapeDtypeStruct((m, d), out_dtype),
        compiler_params=_params("parallel"),
        name="rmsnorm",
    )(x, g.reshape(1, d))


def _mm_kernel(*refs, n_a, has_res, act_pair):
    a_refs = refs[:n_a]
    n_w = 2 if act_pair else 1
    w_refs = refs[n_a:n_a + n_w]
    pos = n_a + n_w
    r_ref = refs[pos] if has_res else None
    pos += int(has_res)
    o_ref = refs[pos]
    wb_refs = refs[pos + 1:pos + 1 + n_w]

    @pl.when(pl.program_id(1) == 0)
    def _():
        for w_ref, wb_ref in zip(w_refs, wb_refs):
            wb_ref[...] = w_ref[...].astype(BF16)

    def contract(wb_ref):
        acc = None
        k0 = 0
        for a_ref in a_refs:
            ka = a_ref.shape[1]
            part = jnp.dot(a_ref[...], wb_ref[k0:k0 + ka, :], preferred_element_type=F32)
            acc = part if acc is None else acc + part
            k0 += ka
        return acc

    if act_pair:
        out = _silu(contract(wb_refs[0])) * contract(wb_refs[1])
    else:
        out = contract(wb_refs[0])
    if has_res:
        out = out + r_ref[...]
    o_ref[...] = out.astype(o_ref.dtype)


def matmul(a_list, w_list, *, w_col0=0, n_cols=None, res=None, out_dtype=F32, tm=1024, tn=512,
           name="matmul"):
    m = a_list[0].shape[0]
    k = sum(a.shape[1] for a in a_list)
    assert all(w.shape[0] == k for w in w_list)
    n_cols = w_list[0].shape[1] - w_col0 if n_cols is None else n_cols
    assert m % tm == 0 and n_cols % tn == 0 and w_col0 % tn == 0
    nb0 = w_col0 // tn
    in_specs = [pl.BlockSpec((tm, a.shape[1]), lambda n, i: (i, 0)) for a in a_list]
    in_specs += [pl.BlockSpec((k, tn), lambda n, i: (0, n + nb0)) for _ in w_list]
    args = list(a_list) + list(w_list)
    if res is not None:
        in_specs.append(pl.BlockSpec((tm, tn), lambda n, i: (i, n)))
        args.append(res)
    kern = functools.partial(_mm_kernel, n_a=len(a_list), has_res=res is not None,
                             act_pair=len(w_list) == 2)
    return pl.pallas_call(
        kern,
        grid=(n_cols // tn, m // tm),
        in_specs=in_specs,
        out_specs=pl.BlockSpec((tm, tn), lambda n, i: (i, n)),
        out_shape=jax.ShapeDtypeStruct((m, n_cols), out_dtype),
        scratch_shapes=[pltpu.VMEM((k, tn), BF16) for _ in w_list],
        compiler_params=_params("parallel", "arbitrary"),
        name=name,
    )(*args)


def _mm_ktiled_kernel(a_ref, w_ref, r_ref, o_ref):
    @pl.when(pl.program_id(2) == 0)
    def _():
        o_ref[...] = r_ref[...]

    o_ref[...] += jnp.dot(a_ref[...], w_ref[...].astype(BF16), preferred_element_type=F32)


def matmul_ktiled_res(a, w, res, *, tm=1024, tn=1024, tk=2048, name="matmul_ktiled"):
    m, k = a.shape
    n = w.shape[1]
    assert m % tm == 0 and n % tn == 0 and k % tk == 0
    return pl.pallas_call(
        _mm_ktiled_kernel,
        grid=(m // tm, n // tn, k // tk),
        in_specs=[pl.BlockSpec((tm, tk), lambda i, j, kk: (i, kk)),
                  pl.BlockSpec((tk, tn), lambda i, j, kk: (kk, j)),
                  pl.BlockSpec((tm, tn), lambda i, j, kk: (i, j))],
        out_specs=pl.BlockSpec((tm, tn), lambda i, j, kk: (i, j)),
        out_shape=jax.ShapeDtypeStruct((m, n), F32),
        compiler_params=_params("parallel", "parallel", "arbitrary"),
        name=name,
    )(a, w, res)


def _moba_kernel(q_ref, k_ref, v_ref, o_ref, kmean_ref, sel_ref, *, blk, nblk, topk, scale, span):
    qb = pl.program_id(2)
    nbp = kmean_ref.shape[0]

    @pl.when(qb == 0)
    def _():
        kmean_ref[...] = jnp.zeros_like(kmean_ref)
        for n in range(nblk):
            kb = k_ref[n * blk:(n + 1) * blk, :].astype(F32)
            kmean_ref[n:n + 1, :] = jnp.mean(kb, axis=0, keepdims=True)

    qs = q_ref[...].astype(F32) * scale
    gate = _nt(kmean_ref[...], qs, precision=HIGHEST)
    bid = lax.broadcasted_iota(jnp.int32, gate.shape, 0)
    t = jnp.where(bid < qb, gate, -jnp.inf)
    sel = jnp.zeros(gate.shape, F32)
    for _ in range(topk):
        mx = jnp.max(t, axis=0, keepdims=True)
        idx = jnp.min(jnp.where(t == mx, bid, nbp), axis=0, keepdims=True)
        pick = (bid == idx) & (mx > -jnp.inf)
        sel = jnp.where(pick, 1.0, sel)
        t = jnp.where(bid == idx, -jnp.inf, t)
    sel_ref[...] = jnp.concatenate([sel, jnp.zeros((LANES - nbp, blk), F32)], axis=0).T

    qsb = qs.astype(BF16)
    row = lax.broadcasted_iota(jnp.int32, (blk, blk), 0)
    col = lax.broadcasted_iota(jnp.int32, (blk, blk), 1)
    ahead = col - row
    for nv in range(span, nblk + 1, span):
        @pl.when((qb >= nv - span) & (qb < nv))
        def _(nv=nv):
            s = _nt(qsb, k_ref[0:nv * blk, :])
            pieces = []
            for kb in range(nv):
                sb = s[:, kb * blk:(kb + 1) * blk]
                if kb >= nv - span:
                    own_limit = jnp.where(qb == kb, 0, -2 * blk)
                    piece = jnp.where(ahead <= own_limit, sb, NEG)
                else:
                    piece = NEG
                pieces.append(jnp.where(sel_ref[:, kb:kb + 1] > 0.5, sb, piece))
            m = jnp.max(functools.reduce(jnp.maximum, pieces), axis=-1, keepdims=True)
            probs = [jnp.exp(piece - m) for piece in pieces]
            l = jnp.sum(functools.reduce(jnp.add, probs), axis=-1, keepdims=True)
            p = jnp.concatenate([pr.astype(BF16) for pr in probs], axis=1)
            o = jnp.dot(p, v_ref[0:nv * blk, :], preferred_element_type=F32)
            o_ref[...] = (o / l).astype(o_ref.dtype)


def moba_attention(qkv, *, nh, dh, span=2):
    bsz, t, _ = qkv.shape
    blk = MOBA_BLOCK
    nblk = t // blk
    nbp = -(-nblk // 8) * 8
    assert t % blk == 0 and nbp <= LANES and dh % LANES == 0 and nblk % span == 0
    kern = functools.partial(_moba_kernel, blk=blk, nblk=nblk, topk=min(MOBA_TOPK, nblk),
                             scale=dh ** -0.5, span=span)
    return pl.pallas_call(
        kern,
        grid=(bsz, nh, nblk),
        in_specs=[pl.BlockSpec((None, blk, dh), lambda b, h, i: (b, i, h)),
                  pl.BlockSpec((None, t, dh), lambda b, h, i: (b, 0, nh + h)),
                  pl.BlockSpec((None, t, dh), lambda b, h, i: (b, 0, 2 * nh + h))],
        out_specs=pl.BlockSpec((None, blk, dh), lambda b, h, i: (b, i, h)),
        out_shape=jax.ShapeDtypeStruct((bsz, t, nh * dh), BF16),
        scratch_shapes=[pltpu.VMEM((nbp, dh), F32),
                        pltpu.VMEM((blk, LANES), F32)],
        compiler_params=_params("parallel", "parallel", "arbitrary"),
        name="moba_attention",
    )(qkv, qkv, qkv)


def _causal_conv(x, buf_ref, w_ref, b_ref, first):
    rows = x.shape[0]

    @pl.when(first)
    def _():
        buf_ref[0:CONV_PAD, :] = jnp.zeros((CONV_PAD, x.shape[1]), F32)

    buf_ref[CONV_PAD:CONV_PAD + rows, :] = x
    y = b_ref[...] + w_ref[CONV_WIDTH - 1:CONV_WIDTH, :] * x
    for kk in range(CONV_WIDTH - 1):
        off = CONV_PAD - (CONV_WIDTH - 1) + kk
        y = y + w_ref[kk:kk + 1, :] * buf_ref[off:off + rows, :]
    buf_ref[0:CONV_PAD, :] = x[rows - CONV_PAD:rows, :]
    return y


def _lru_kernel(xr_ref, gr_ref, cw_ref, cb_ref, wa_ref, ba_ref, wx_ref, bx_ref, lam_ref, o_ref,
                xbuf_ref, h_ref, *, nblocks, bdim):
    ti = pl.program_id(2)
    rows = xr_ref.shape[0]

    @pl.when(ti == 0)
    def _():
        h_ref[...] = jnp.zeros_like(h_ref)

    x = _causal_conv(xr_ref[...], xbuf_ref, cw_ref, cb_ref, ti == 0)
    ga, gx = [], []
    for n in range(nblocks):
        xb = x[:, n * bdim:(n + 1) * bdim].astype(BF16)
        ga.append(jnp.dot(xb, wa_ref[n].astype(BF16), preferred_element_type=F32))
        gx.append(jnp.dot(xb, wx_ref[n].astype(BF16), preferred_element_type=F32))
    gate_a = jnp.concatenate(ga, axis=-1) + ba_ref[...]
    gate_x = jnp.concatenate(gx, axis=-1) + bx_ref[...]
    log_a = -LRU_C * _sigmoid(gate_a) * _softplus(-lam_ref[...])
    a = jnp.exp(log_a)
    u = jnp.sqrt(1.0 - jnp.exp(2.0 * log_a)) * _sigmoid(gate_x) * x

    sub = jnp.bitwise_and(lax.broadcasted_iota(jnp.int32, a.shape, 0), SUBLANES - 1)
    s = 1
    while s < SUBLANES:
        keep = sub >= s
        a_prev = jnp.where(keep, pltpu.roll(a, s, 0), 1.0)
        u_prev = jnp.where(keep, pltpu.roll(u, s, 0), 0.0)
        u = a * u_prev + u
        a = a * a_prev
        s *= 2
    carry = h_ref[0:1, :]
    groups = []
    for gi in range(rows // SUBLANES):
        lo = gi * SUBLANES
        hg = u[lo:lo + SUBLANES] + a[lo:lo + SUBLANES] * carry
        groups.append(hg)
        carry = hg[SUBLANES - 1:SUBLANES, :]
    h = jnp.concatenate(groups, axis=0)
    h_ref[0:1, :] = carry

    g = gr_ref[...]
    gelu = 0.5 * g * (1.0 + jnp.tanh(math.sqrt(2.0 / math.pi) * (g + 0.044715 * g * g * g)))
    o_ref[...] = (h * gelu).astype(o_ref.dtype)


def conv_rglru(xg, conv_w, conv_b, w_a, b_a, w_x, b_x, lam, *, tt=256, tc=512):
    bsz, t, w2 = xg.shape
    w = w2 // 2
    nblocks_all, bdim, _ = w_a.shape
    nb = tc // bdim
    assert t % tt == 0 and w % tc == 0 and tc % bdim == 0
    row = lambda v: v.reshape(1, w)
    vec_spec = pl.BlockSpec((1, tc), lambda b, c, i: (0, c))
    gate_spec = pl.BlockSpec((nb, bdim, bdim), lambda b, c, i: (c, 0, 0))
    kern = functools.partial(_lru_kernel, nblocks=nb, bdim=bdim)
    return pl.pallas_call(
        kern,
        grid=(bsz, w // tc, t // tt),
        in_specs=[pl.BlockSpec((None, tt, tc), lambda b, c, i: (b, i, c)),
                  pl.BlockSpec((None, tt, tc), lambda b, c, i: (b, i, w // tc + c)),
                  pl.BlockSpec((CONV_WIDTH, tc), lambda b, c, i: (0, c)),
                  vec_spec, gate_spec, vec_spec, gate_spec, vec_spec, vec_spec],
        out_specs=pl.BlockSpec((None, tt, tc), lambda b, c, i: (b, i, c)),
        out_shape=jax.ShapeDtypeStruct((bsz, t, w), BF16),
        scratch_shapes=[pltpu.VMEM((CONV_PAD + tt, tc), F32),
                        pltpu.VMEM((8, tc), F32)],
        compiler_params=_params("parallel", "parallel", "arbitrary"),
        name="conv_rglru",
    )(xg, xg, conv_w, row(conv_b), w_a, row(b_a), w_x, row(b_x), row(lam))


def _gla_kernel(q_ref, k_ref, v_ref, r_ref, lr_ref, wa_ref, ba_ref, g_ref, o_ref, st_ref, *,
                chunk, scale, hps, dk, dv):
    ci = pl.program_id(2)

    @pl.when(ci == 0)
    def _():
        st_ref[...] = jnp.zeros_like(st_ref)

    z = jnp.dot(lr_ref[...].astype(BF16), wa_ref[...].astype(BF16),
                preferred_element_type=F32) + ba_ref[...]
    g = (jnp.minimum(z, 0.0) - jnp.log(1.0 + jnp.exp(-jnp.abs(z)))) / GLA_TAU
    row = lax.broadcasted_iota(jnp.int32, (chunk, chunk), 0)
    col = lax.broadcasted_iota(jnp.int32, (chunk, chunk), 1)
    sub0 = row - jnp.bitwise_and(row, GLA_SUB - 1)
    sums = jnp.concatenate([jnp.where(col <= row, 1.0, 0.0), jnp.where(col < sub0, 1.0, 0.0)],
                           axis=0).astype(BF16)
    bb = _dot_exact01(sums, g, ones_on_left=True)
    jrow = lax.broadcasted_iota(jnp.int32, (chunk, dk), 0)
    srow = lax.broadcasted_iota(jnp.int32, (GLA_SUB, chunk), 0)
    scol = lax.broadcasted_iota(jnp.int32, (GLA_SUB, chunk), 1)

    for h in range(hps):
        b = bb[0:chunk, h * dk:(h + 1) * dk]
        b_sub = bb[chunk:2 * chunk, h * dk:(h + 1) * dk]
        q = q_ref[:, h * dk:(h + 1) * dk].astype(F32) * scale
        k = k_ref[:, h * dk:(h + 1) * dk].astype(F32)
        v = v_ref[:, h * dv:(h + 1) * dv]
        b_last = b[chunk - 1:chunk, :]

        o = _nt((q * jnp.exp(b)).astype(BF16), st_ref[h].astype(BF16))

        parts = []
        for i in range(chunk // GLA_SUB):
            lo, hi = i * GLA_SUB, (i + 1) * GLA_SUB
            ref = b_sub[lo:lo + 1, :]
            qi = (q[lo:hi] * jnp.exp(b[lo:hi] - ref)).astype(BF16)
            ki = (k * jnp.exp(jnp.where(jrow < hi, ref - b, 0.0))).astype(BF16)
            att = _nt(qi, ki)
            att = jnp.where(scol <= srow + lo, att, 0.0)
            parts.append(jnp.dot(att.astype(BF16), v, preferred_element_type=F32))
        o = o + jnp.concatenate(parts, axis=0)

        kd = (k * jnp.exp(b_last - b)).astype(BF16)
        upd = lax.dot_general(v, kd, (((0,), (0,)), ((), ())), preferred_element_type=F32)
        st_ref[h] = st_ref[h] * jnp.exp(b_last) + upd

        y = o * lax.rsqrt(jnp.mean(o * o, axis=-1, keepdims=True) + NORM_EPS) * g_ref[...]
        gate = _silu(r_ref[:, h * dv:(h + 1) * dv].astype(F32))
        o_ref[:, h * dv:(h + 1) * dv] = (y * gate).astype(o_ref.dtype)


def gla(p1a, sm, w_a2p, b_a, norm, *, nh, dk, dv, chunk=128, hps=4):
    bsz, t, _ = p1a.shape
    hps = min(hps, nh)
    assert t % chunk == 0 and chunk % GLA_SUB == 0 and nh % hps == 0
    kq = nh * dk
    gk, gv = hps * dk, hps * dv
    koff, voff, roff = kq // gk, 2 * kq // gv, (2 * kq + nh * dv) // gv
    kern = functools.partial(_gla_kernel, chunk=chunk, scale=dk ** -0.5, hps=hps, dk=dk, dv=dv)
    return pl.pallas_call(
        kern,
        grid=(bsz, nh // hps, t // chunk),
        in_specs=[pl.BlockSpec((None, chunk, gk), lambda b, h, c: (b, c, h)),
                  pl.BlockSpec((None, chunk, gk), lambda b, h, c: (b, c, koff + h)),
                  pl.BlockSpec((None, chunk, gv), lambda b, h, c: (b, c, voff + h)),
                  pl.BlockSpec((None, chunk, gv), lambda b, h, c: (b, c, roff + h)),
                  pl.BlockSpec((None, chunk, LANES), lambda b, h, c: (b, c, 0)),
                  pl.BlockSpec((LANES, gk), lambda b, h, c: (0, h)),
                  pl.BlockSpec((1, gk), lambda b, h, c: (0, h)),
                  pl.BlockSpec((1, dv), lambda b, h, c: (0, 0))],
        out_specs=pl.BlockSpec((None, chunk, gv), lambda b, h, c: (b, c, h)),
        out_shape=jax.ShapeDtypeStruct((bsz, t, nh * dv), BF16),
        scratch_shapes=[pltpu.VMEM((hps, dv, dk), F32)],
        compiler_params=_params("parallel", "parallel", "arbitrary"),
        name="gla",
    )(p1a, p1a, p1a, p1a, sm, w_a2p, b_a.reshape(1, kq), norm.reshape(1, dv))


def _ssd_kernel(z_ref, xs_ref, bm_ref, cm_ref, dt_ref, cwx_ref, cwb_ref, cwc_ref, cbx_ref, cbb_ref,
                cbc_ref, dtb_ref, alog_ref, dfull_ref, norm_ref, o_ref,
                xbuf_ref, bbuf_ref, cbuf_ref, st_ref, *, chunk, hpg, hd):
    ci = pl.program_id(2)
    first = ci == 0

    @pl.when(first)
    def _():
        st_ref[...] = jnp.zeros_like(st_ref)

    xs = _silu(_causal_conv(xs_ref[...], xbuf_ref, cwx_ref, cbx_ref, first))
    bm = _silu(_causal_conv(bm_ref[...], bbuf_ref, cwb_ref, cbb_ref, first))
    cm = _silu(_causal_conv(cm_ref[...], cbuf_ref, cwc_ref, cbc_ref, first))
    width = hpg * hd

    dt = _softplus(dt_ref[...] + dtb_ref[...])
    da = dt * (-jnp.exp(alog_ref[...]))
    row = lax.broadcasted_iota(jnp.int32, (chunk, chunk), 0)
    col = lax.broadcasted_iota(jnp.int32, (chunk, chunk), 1)
    causal = col <= row
    tril = jnp.where(causal, 1.0, 0.0).astype(BF16)
    cs = _dot_exact01(tril, da, ones_on_left=True)
    cs_t = cs.T

    erow = lax.broadcasted_iota(jnp.int32, (LANES, width), 0)
    ecol = lax.broadcasted_iota(jnp.int32, (LANES, width), 1)
    expand = jnp.where((ecol >= erow * hd) & (ecol < (erow + 1) * hd), 1.0, 0.0).astype(BF16)
    full = _dot_exact01(expand, jnp.concatenate([cs, dt], axis=0), ones_on_left=False)
    cs_full = full[0:chunk]
    dt_full = full[chunk:2 * chunk]
    cs_last = cs_full[chunk - 1:chunk, :]

    xd = xs * dt_full
    xd_b = xd.astype(BF16)
    bm_b = bm.astype(BF16)
    cm_b = cm.astype(BF16)
    cb = _nt(cm_b, bm_b)

    lane = lax.broadcasted_iota(jnp.int32, (chunk, 2 * hd), 1)
    slabs = []
    for j in range(hpg // 2):
        xpair = xd_b[:, 2 * j * hd:(2 * j + 2) * hd]
        acc = None
        for half in range(2):
            h = 2 * j + half
            seg = cs[:, h:h + 1] - cs_t[h:h + 1, :]
            lmat = jnp.exp(jnp.where(causal, seg, NEG))
            mh = (cb * lmat).astype(BF16)
            mine = (lane >= half * hd) & (lane < (half + 1) * hd)
            part = jnp.dot(mh, jnp.where(mine, xpair, jnp.zeros_like(xpair)),
                           preferred_element_type=F32)
            acc = part if acc is None else acc + part
        slabs.append(acc)
    y = jnp.concatenate(slabs, axis=-1)

    st = st_ref[...]
    y = y + jnp.dot(cm_b, st.astype(BF16), preferred_element_type=F32) * jnp.exp(cs_full)
    xdd = (xd * jnp.exp(cs_last - cs_full)).astype(BF16)
    upd = lax.dot_general(bm_b, xdd, (((0,), (0,)), ((), ())), preferred_element_type=F32)
    st_ref[...] = st * jnp.exp(cs_last) + upd

    y = y + dfull_ref[...] * xs
    y = y * _silu(z_ref[...])
    y = y * lax.rsqrt(jnp.mean(y * y, axis=-1, keepdims=True) + NORM_EPS) * norm_ref[...]
    o_ref[...] = y.astype(o_ref.dtype)


def ssd(p1b, sm, conv_w, conv_b, dt_bias, a_log, d_skip, norm, *, nheads, hd, groups, nstate,
        chunk=256):
    bsz, t, _ = p1b.shape
    width = nheads * hd
    gw = width // groups
    hpg = nheads // groups
    assert t % chunk == 0 and nstate == LANES and gw % LANES == 0 and hpg % 2 == 0
    xoff = width // gw
    boff = 2 * width // nstate
    coff = boff + groups
    cxo, cbo, cco = 0, width // nstate, width // nstate + groups
    padl = lambda v: jnp.pad(v.reshape(groups, 1, hpg), ((0, 0), (0, 0), (0, LANES - hpg)))
    dfull = jnp.repeat(d_skip.reshape(groups, hpg), hd, axis=1).reshape(groups, 1, gw)
    cb2 = conv_b.reshape(1, -1)
    g3 = lambda b, g, c: (g, 0, 0)
    kern = functools.partial(_ssd_kernel, chunk=chunk, hpg=hpg, hd=hd)
    return pl.pallas_call(
        kern,
        grid=(bsz, groups, t // chunk),
        in_specs=[pl.BlockSpec((None, chunk, gw), lambda b, g, c: (b, c, g)),
                  pl.BlockSpec((None, chunk, gw), lambda b, g, c: (b, c, xoff + g)),
                  pl.BlockSpec((None, chunk, nstate), lambda b, g, c: (b, c, boff + g)),
                  pl.BlockSpec((None, chunk, nstate), lambda b, g, c: (b, c, coff + g)),
                  pl.BlockSpec((None, chunk, LANES), lambda b, g, c: (b, c, 1 + g)),
                  pl.BlockSpec((CONV_WIDTH, gw), lambda b, g, c: (0, cxo + g)),
                  pl.BlockSpec((CONV_WIDTH, nstate), lambda b, g, c: (0, cbo + g)),
                  pl.BlockSpec((CONV_WIDTH, nstate), lambda b, g, c: (0, cco + g)),
                  pl.BlockSpec((1, gw), lambda b, g, c: (0, cxo + g)),
                  pl.BlockSpec((1, nstate), lambda b, g, c: (0, cbo + g)),
                  pl.BlockSpec((1, nstate), lambda b, g, c: (0, cco + g)),
                  pl.BlockSpec((None, 1, LANES), g3),
                  pl.BlockSpec((None, 1, LANES), g3),
                  pl.BlockSpec((None, 1, gw), g3),
                  pl.BlockSpec((1, gw), lambda b, g, c: (0, g))],
        out_specs=pl.BlockSpec((None, chunk, gw), lambda b, g, c: (b, c, g)),
        out_shape=jax.ShapeDtypeStruct((bsz, t, width), BF16),
        scratch_shapes=[pltpu.VMEM((CONV_PAD + chunk, gw), F32),
                        pltpu.VMEM((CONV_PAD + chunk, nstate), F32),
                        pltpu.VMEM((CONV_PAD + chunk, nstate), F32),
                        pltpu.VMEM((nstate, gw), F32)],
        compiler_params=_params("parallel", "parallel", "arbitrary"),
        name="ssd",
    )(p1b, p1b, p1b, p1b, sm, conv_w, conv_w, conv_w, cb2, cb2, cb2,
      padl(dt_bias), padl(a_log), dfull, norm.reshape(1, width))


def _norm_router_kernel(x_ref, g_ref, r_ref, h_ref, route_ref, *, n_exp):
    x = x_ref[...]
    y = x * lax.rsqrt(jnp.mean(x * x, axis=-1, keepdims=True) + NORM_EPS) * g_ref[...]
    h_ref[...] = y
    logits = jnp.dot(y, r_ref[...], precision=HIGHEST, preferred_element_type=F32)
    lane = lax.broadcasted_iota(jnp.int32, logits.shape, 1)
    t = jnp.where(lane < n_exp, logits, -jnp.inf)
    m1 = jnp.max(t, axis=-1, keepdims=True)
    i1 = jnp.min(jnp.where(t == m1, lane, LANES), axis=-1, keepdims=True)
    t2 = jnp.where(lane == i1, -jnp.inf, t)
    m2 = jnp.max(t2, axis=-1, keepdims=True)
    i2 = jnp.min(jnp.where(t2 == m2, lane, LANES), axis=-1, keepdims=True)
    e = jnp.exp(m2 - m1)
    w1 = 1.0 / (1.0 + e)
    w2 = e / (1.0 + e)
    route = jnp.where(lane == 0, i1.astype(F32),
                      jnp.where(lane == 1, i2.astype(F32),
                                jnp.where(lane == 2, w1, jnp.where(lane == 3, w2, 0.0))))
    route_ref[...] = route


def norm_router(x, g, router, tm=256):
    m, d = x.shape
    n_exp = router.shape[1]
    rp = jnp.pad(router, ((0, 0), (0, LANES - n_exp)))
    return pl.pallas_call(
        functools.partial(_norm_router_kernel, n_exp=n_exp),
        grid=(m // tm,),
        in_specs=[pl.BlockSpec((tm, d), lambda i: (i, 0)),
                  pl.BlockSpec((1, d), lambda i: (0, 0)),
                  pl.BlockSpec((d, LANES), lambda i: (0, 0))],
        out_specs=[pl.BlockSpec((tm, d), lambda i: (i, 0)),
                   pl.BlockSpec((tm, LANES), lambda i: (i, 0))],
        out_shape=[jax.ShapeDtypeStruct((m, d), F32), jax.ShapeDtypeStruct((m, LANES), F32)],
        compiler_params=_params("parallel"),
        name="norm_router",
    )(x, g.reshape(1, d), rp)


def _row_gather(idx_ref, base, src_hbm, dst_ref, sem, rows):
    def start(i, c):
        for j in range(GATHER_UNROLL):
            r = i * GATHER_UNROLL + j
            pltpu.make_async_copy(src_hbm.at[pl.ds(idx_ref[base + r], 1)],
                                  dst_ref.at[pl.ds(r, 1)], sem).start(priority=j % 2)
        return c

    lax.fori_loop(0, rows // GATHER_UNROLL, start, 0)
    pltpu.make_async_copy(src_hbm.at[pl.ds(0, rows)], dst_ref, sem).wait()


def _dispatch_kernel(tok_ref, h_hbm, o_ref, buf_ref, sem):
    rows = buf_ref.shape[0]
    _row_gather(tok_ref, pl.program_id(0) * rows, h_hbm, buf_ref, sem, rows)
    o_ref[...] = buf_ref[...].astype(o_ref.dtype)


def moe_dispatch(row_token, h, tm=MOE_ROW_TILE):
    r = row_token.shape[0]
    d = h.shape[1]
    assert tm % GATHER_UNROLL == 0
    return pl.pallas_call(
        _dispatch_kernel,
        grid_spec=pltpu.PrefetchScalarGridSpec(
            num_scalar_prefetch=1,
            grid=(r // tm,),
            in_specs=[pl.BlockSpec(memory_space=pl.ANY)],
            out_specs=pl.BlockSpec((tm, d), lambda i, tok: (i, 0)),
            scratch_shapes=[pltpu.VMEM((tm, d), F32), pltpu.SemaphoreType.DMA(())]),
        out_shape=jax.ShapeDtypeStruct((r, d), BF16),
        compiler_params=_params("arbitrary"),
        name="moe_dispatch",
    )(row_token, h)


def _moe_mm_kernel(te_ref, tv_ref, x_ref, *refs, act_pair):
    n_w = 2 if act_pair else 1
    w_refs = refs[:n_w]
    o_ref = refs[n_w]
    r = pl.program_id(1)

    live = tv_ref[r]
    tm = x_ref.shape[0]
    half = tm // 2

    def compute(nrows):
        x = x_ref[0:nrows, :]
        out = jnp.dot(x, w_refs[0][...].astype(BF16), preferred_element_type=F32)
        if act_pair:
            out = _silu(out) * jnp.dot(x, w_refs[1][...].astype(BF16), preferred_element_type=F32)
        o_ref[0:nrows, :] = out.astype(o_ref.dtype)

    @pl.when(live > half)
    def _():
        compute(tm)

    @pl.when((live > 0) & (live <= half))
    def _():
        compute(half)
        o_ref[half:tm, :] = jnp.zeros((tm - half, o_ref.shape[1]), o_ref.dtype)

    @pl.when(live == 0)
    def _():
        o_ref[...] = jnp.zeros_like(o_ref)


def moe_matmul(tile_expert, tile_rows, x, w_list, *, out_dtype, tm=MOE_ROW_TILE, tn=256,
               name="moe_matmul"):
    r, k = x.shape
    n = w_list[0].shape[2]
    assert r % tm == 0 and n % tn == 0
    kern = functools.partial(_moe_mm_kernel, act_pair=len(w_list) == 2)
    w_spec = pl.BlockSpec((None, k, tn), lambda j, i, te, tv: (te[i], 0, j))
    return pl.pallas_call(
        kern,
        grid_spec=pltpu.PrefetchScalarGridSpec(
            num_scalar_prefetch=2,
            grid=(n // tn, r // tm),
            in_specs=[pl.BlockSpec((tm, k), lambda j, i, te, tv: (i, 0))] + [w_spec] * len(w_list),
            out_specs=pl.BlockSpec((tm, tn), lambda j, i, te, tv: (i, j))),
        out_shape=jax.ShapeDtypeStruct((r, n), out_dtype),
        compiler_params=_params("parallel", "arbitrary"),
        name=name,
    )(tile_expert, tile_rows, x, *w_list)


def _combine_kernel(d0_ref, d1_ref, x_ref, route_ref, g_ref, y_hbm, o_ref, buf0_ref, buf1_ref, sems):
    rows = x_ref.shape[0]
    base = pl.program_id(0) * rows
    _row_gather(d0_ref, base, y_hbm, buf0_ref, sems.at[0], rows)
    _row_gather(d1_ref, base, y_hbm, buf1_ref, sems.at[1], rows)
    w0 = route_ref[:, 2:3]
    w1 = route_ref[:, 3:4]
    x = x_ref[...] + w0 * buf0_ref[...] + w1 * buf1_ref[...]
    y = x * lax.rsqrt(jnp.mean(x * x, axis=-1, keepdims=True) + NORM_EPS)
    o_ref[...] = y * g_ref[...]


def moe_combine_norm(dest0, dest1, x, route, y, g, tm=256):
    m, d = x.shape
    assert tm % GATHER_UNROLL == 0
    return pl.pallas_call(
        _combine_kernel,
        grid_spec=pltpu.PrefetchScalarGridSpec(
            num_scalar_prefetch=2,
            grid=(m // tm,),
            in_specs=[pl.BlockSpec((tm, d), lambda i, a, b: (i, 0)),
                      pl.BlockSpec((tm, LANES), lambda i, a, b: (i, 0)),
                      pl.BlockSpec((1, d), lambda i, a, b: (0, 0)),
                      pl.BlockSpec(memory_space=pl.ANY)],
            out_specs=pl.BlockSpec((tm, d), lambda i, a, b: (i, 0)),
            scratch_shapes=[pltpu.VMEM((tm, d), F32), pltpu.VMEM((tm, d), F32),
                            pltpu.SemaphoreType.DMA((2,))]),
        out_shape=jax.ShapeDtypeStruct((m, d), F32),
        compiler_params=_params("arbitrary"),
        name="moe_combine_norm",
    )(dest0, dest1, x, route, g.reshape(1, d), y)


def _moe_plan(ids, tm):
    n_tok = ids.shape[0]
    e_flat = ids.reshape(-1)
    onehot = (e_flat[:, None] == jnp.arange(N_EXPERTS, dtype=jnp.int32)[None, :]).astype(jnp.int32)
    csum = jnp.cumsum(onehot, axis=0)
    pos = jnp.sum(csum * onehot, axis=1) - 1
    counts = csum[-1]
    padded = ((counts + tm - 1) // tm) * tm
    gend = jnp.cumsum(padded)
    gstart = gend - padded
    dest = gstart[e_flat] + pos
    n_rows = 2 * n_tok + N_EXPERTS * tm
    row_token = jnp.zeros((n_rows,), jnp.int32).at[dest].set(
        jnp.arange(2 * n_tok, dtype=jnp.int32) // 2)
    tile_start = jnp.arange(n_rows // tm, dtype=jnp.int32) * tm
    last_start = jnp.maximum(gend[-1] - tm, 0)
    probe = jnp.minimum(tile_start, last_start)
    tile_expert = jnp.minimum(jnp.sum((probe[:, None] >= gend[None, :]).astype(jnp.int32), axis=1),
                              N_EXPERTS - 1).astype(jnp.int32)
    live_end = (gstart + counts)[tile_expert]
    tile_rows = jnp.where(tile_start < gend[-1], jnp.clip(live_end - tile_start, 0, tm), 0)
    dest2 = dest.reshape(n_tok, 2).astype(jnp.int32)
    return row_token, tile_expert, tile_rows.astype(jnp.int32), dest2[:, 0], dest2[:, 1]


def kernel(x, ev_norm1, ev_w_in, ev_lru_conv_w, ev_lru_conv_b, ev_lru_w_a, ev_lru_b_a, ev_lru_w_x, ev_lru_b_x, ev_lru_lambda, ev_w_out, ev_norm2, ev_ffn_w1, ev_ffn_w3, ev_ffn_w2, od_norm1, od_w_in, od_gla_w_a2, od_gla_b_a, od_gla_norm, od_ssd_conv_w, od_ssd_conv_b, od_ssd_dt_bias, od_ssd_a_log, od_ssd_d, od_ssd_norm, od_w_out, od_norm2, od_router, od_moe_w1, od_moe_w3, od_moe_w2, final_norm):
    bsz, t, d = x.shape
    n_tok = bsz * t
    xf = x.reshape(n_tok, d)

    lru_w = ev_lru_lambda.shape[1]
    moba_w = (ev_w_in.shape[2] - 2 * lru_w) // 3
    moba_dh = LANES
    h = rmsnorm(xf, ev_norm1[0])
    qkv = matmul([h], [ev_w_in[0]], w_col0=0, n_cols=3 * moba_w, out_dtype=BF16, name="l0_in_qkv")
    xg = matmul([h], [ev_w_in[0]], w_col0=3 * moba_w, n_cols=2 * lru_w, out_dtype=F32, name="l0_in_lru")
    att = moba_attention(qkv.reshape(bsz, t, 3 * moba_w), nh=moba_w // moba_dh, dh=moba_dh)
    rec = conv_rglru(xg.reshape(bsz, t, 2 * lru_w), ev_lru_conv_w[0], ev_lru_conv_b[0], ev_lru_w_a[0],
                     ev_lru_b_a[0], ev_lru_w_x[0], ev_lru_b_x[0], ev_lru_lambda[0])
    xf = matmul([att.reshape(n_tok, moba_w), rec.reshape(n_tok, lru_w)], [ev_w_out[0]], res=xf,
                name="l0_out")
    h = rmsnorm(xf, ev_norm2[0])
    gact = matmul([h], [ev_ffn_w1[0], ev_ffn_w3[0]], out_dtype=BF16, tm=1024, tn=256, name="l0_ffn_up")
    xf = matmul_ktiled_res(gact, ev_ffn_w2[0], xf, name="l0_ffn_down")

    gla_key = od_gla_w_a2.shape[2]
    gla_rank = od_gla_w_a2.shape[1]
    gla_dv = od_gla_norm.shape[1]
    gla_heads = gla_key // (gla_dv // 2)
    gla_val = gla_heads * gla_dv
    ssd_heads = od_ssd_a_log.shape[1]
    ssd_width = od_ssd_norm.shape[1]
    ssd_conv_dim = od_ssd_conv_w.shape[2]
    ssd_groups = 4
    ssd_state = (ssd_conv_dim - ssd_width) // (2 * ssd_groups)
    hpg = ssd_heads // ssd_groups
    w_in1 = od_w_in[0]
    c_lr = 2 * gla_key + 2 * gla_val
    c_z = c_lr + gla_rank
    c_dt = c_z + ssd_width + ssd_conv_dim
    zpad = lambda n: jnp.zeros((d, n), F32)
    w_small = jnp.concatenate(
        [w_in1[:, c_lr:c_z], zpad(LANES - gla_rank)]
        + [blk for g in range(ssd_groups)
           for blk in (w_in1[:, c_dt + g * hpg:c_dt + (g + 1) * hpg], zpad(LANES - hpg))], axis=1)
    w_zxbc = w_in1[:, c_z:c_dt].astype(BF16)

    h = rmsnorm(xf, od_norm1[0])
    p1a = matmul([h], [w_in1], w_col0=0, n_cols=c_lr, out_dtype=BF16, name="l1_in_gla")
    p1b = matmul([h], [w_zxbc], out_dtype=F32, name="l1_in_ssd")
    sm = matmul([h], [w_small], out_dtype=F32, tn=w_small.shape[1], name="l1_in_small")
    w_a2p = jnp.pad(od_gla_w_a2[0], ((0, LANES - gla_rank), (0, 0)))
    o_gla = gla(p1a.reshape(bsz, t, c_lr), sm.reshape(bsz, t, -1), w_a2p, od_gla_b_a[0], od_gla_norm[0],
                nh=gla_heads, dk=gla_key // gla_heads, dv=gla_dv)
    y_ssd = ssd(p1b.reshape(bsz, t, -1), sm.reshape(bsz, t, -1), od_ssd_conv_w[0], od_ssd_conv_b[0],
                od_ssd_dt_bias[0], od_ssd_a_log[0], od_ssd_d[0], od_ssd_norm[0],
                nheads=ssd_heads, hd=ssd_width // ssd_heads, groups=ssd_groups, nstate=ssd_state)
    xf = matmul([o_gla.reshape(n_tok, gla_val), y_ssd.reshape(n_tok, ssd_width)], [od_w_out[0]], res=xf,
                name="l1_out")

    hf, route = norm_router(xf, od_norm2[0], od_router[0])
    ids = route[:, 0:2].astype(jnp.int32)
    row_token, tile_expert, tile_rows, dest0, dest1 = _moe_plan(ids, MOE_ROW_TILE)
    xs = moe_dispatch(row_token, hf)
    gact = moe_matmul(tile_expert, tile_rows, xs, [od_moe_w1[0], od_moe_w3[0]], out_dtype=BF16,
                      tn=MOE_COL_TILE, name="moe_up")
    ye = moe_matmul(tile_expert, tile_rows, gact, [od_moe_w2[0]], out_dtype=F32,
                    tn=MOE_COL_TILE, name="moe_down")
    out = moe_combine_norm(dest0, dest1, xf, route, ye, final_norm)
    return out.reshape(bsz, t, d)
```

```python
import functools
import math

import jax
import jax.numpy as jnp
from jax import lax
from jax.experimental import pallas as pl
from jax.experimental.pallas import tpu as pltpu

F32 = jnp.float32
BF16 = jnp.bfloat16
HIGHEST = lax.Precision.HIGHEST

NORM_EPS = 1e-6
NEG = -1e30
LANES = 128
SUBLANES = 8
VMEM_LIMIT = 56 * 1024 * 1024

CONV_WIDTH = 4
CONV_PAD = 8
MOBA_BLOCK = 256
MOBA_TOPK = 3
LRU_C = 8.0
GLA_TAU = 16.0
GLA_SUB = 16
N_EXPERTS = 8
MOE_ROW_TILE = 512
MOE_COL_TILE = 512
GATHER_UNROLL = 8


def _nt(a, b, precision=None):
    return lax.dot_general(a, b, (((1,), (1,)), ((), ())), precision=precision,
                           preferred_element_type=F32)


def _dot_exact01(ones, x, *, ones_on_left):
    hi = x.astype(BF16)
    r1 = x - hi.astype(F32)
    mid = r1.astype(BF16)
    lo = (r1 - mid.astype(F32)).astype(BF16)
    out = None
    for term in (hi, mid, lo):
        part = (jnp.dot(ones, term, preferred_element_type=F32) if ones_on_left
                else jnp.dot(term, ones, preferred_element_type=F32))
        out = part if out is None else out + part
    return out


def _sigmoid(x):
    return 1.0 / (1.0 + jnp.exp(-x))


def _silu(x):
    return x * _sigmoid(x)


def _softplus(x):
    return jnp.maximum(x, 0.0) + jnp.log(1.0 + jnp.exp(-jnp.abs(x)))


def _params(*sem):
    return pltpu.CompilerParams(dimension_semantics=sem, vmem_limit_bytes=VMEM_LIMIT)


def _rmsnorm_kernel(x_ref, g_ref, o_ref):
    x = x_ref[...]
    y = x * lax.rsqrt(jnp.mean(x * x, axis=-1, keepdims=True) + NORM_EPS)
    o_ref[...] = (y * g_ref[...]).astype(o_ref.dtype)


def rmsnorm(x, g, out_dtype=BF16, tm=256):
    m, d = x.shape
    return pl.pallas_call(
        _rmsnorm_kernel,
        grid=(m // tm,),
        in_specs=[pl.BlockSpec((tm, d), lambda i: (i, 0)),
                  pl.BlockSpec((1, d), lambda i: (0, 0))],
        out_specs=pl.BlockSpec((tm, d), lambda i: (i, 0)),
        out_shape=jax.ShapeDtypeStruct((m, d), out_dtype),
        compiler_params=_params("parallel"),
        name="rmsnorm",
    )(x, g.reshape(1, d))


def _mm_kernel(*refs, n_a, has_res, act_pair, lane_shift):
    a_refs = refs[:n_a]
    n_w = 2 if act_pair else 1
    w_refs = refs[n_a:n_a + n_w]
    pos = n_a + n_w
    wn_ref = refs[pos] if lane_shift else None
    pos += int(bool(lane_shift))
    r_ref = refs[pos] if has_res else None
    pos += int(has_res)
    o_ref = refs[pos]
    wb_refs = refs[pos + 1:pos + 1 + n_w]

    @pl.when(pl.program_id(1) == 0)
    def _():
        for w_ref, wb_ref in zip(w_refs, wb_refs):
            if lane_shift:
                tn = w_ref.shape[1]
                rolled = pltpu.roll(w_ref[...], tn - lane_shift, 1)
                tail = pltpu.roll(wn_ref[...], LANES - lane_shift, 1)
                lane = lax.broadcasted_iota(jnp.int32, tail.shape, 1)
                last = jnp.where(lane < LANES - lane_shift, rolled[:, tn - LANES:tn], tail)
                wb_ref[:, 0:tn - LANES] = rolled[:, 0:tn - LANES].astype(BF16)
                wb_ref[:, tn - LANES:tn] = last.astype(BF16)
            else:
                wb_ref[...] = w_ref[...].astype(BF16)

    def contract(wb_ref):
        acc = None
        k0 = 0
        for a_ref in a_refs:
            ka = a_ref.shape[1]
            part = jnp.dot(a_ref[...], wb_ref[k0:k0 + ka, :], preferred_element_type=F32)
            acc = part if acc is None else acc + part
            k0 += ka
        return acc

    if act_pair:
        out = _silu(contract(wb_refs[0])) * contract(wb_refs[1])
    else:
        out = contract(wb_refs[0])
    if has_res:
        out = out + r_ref[...]
    o_ref[...] = out.astype(o_ref.dtype)


def matmul(a_list, w_list, *, w_col0=0, n_cols=None, res=None, out_dtype=F32, tm=1024, tn=512,
           name="matmul"):
    m = a_list[0].shape[0]
    k = sum(a.shape[1] for a in a_list)
    assert all(w.shape[0] == k for w in w_list)
    n_cols = w_list[0].shape[1] - w_col0 if n_cols is None else n_cols
    lane_shift = w_col0 % LANES
    w_base = w_col0 - lane_shift
    assert m % tm == 0 and n_cols % tn == 0 and w_base % tn == 0
    assert not lane_shift or len(w_list) == 1
    nb0 = w_base // tn
    in_specs = [pl.BlockSpec((tm, a.shape[1]), lambda n, i: (i, 0)) for a in a_list]
    in_specs += [pl.BlockSpec((k, tn), lambda n, i: (0, n + nb0)) for _ in w_list]
    args = list(a_list) + list(w_list)
    if lane_shift:
        in_specs.append(pl.BlockSpec((k, LANES), lambda n, i: (0, (n + nb0 + 1) * (tn // LANES))))
        args.append(w_list[0])
    if res is not None:
        in_specs.append(pl.BlockSpec((tm, tn), lambda n, i: (i, n)))
        args.append(res)
    kern = functools.partial(_mm_kernel, n_a=len(a_list), has_res=res is not None,
                             act_pair=len(w_list) == 2, lane_shift=lane_shift)
    return pl.pallas_call(
        kern,
        grid=(n_cols // tn, m // tm),
        in_specs=in_specs,
        out_specs=pl.BlockSpec((tm, tn), lambda n, i: (i, n)),
        out_shape=jax.ShapeDtypeStruct((m, n_cols), out_dtype),
        scratch_shapes=[pltpu.VMEM((k, tn), BF16) for _ in w_list],
        compiler_params=_params("parallel", "arbitrary"),
        name=name,
    )(*args)


def _mm_rows_kernel(a_ref, w1_ref, w3_ref, o_ref):
    a = a_ref[...]
    gate = jnp.dot(a, w1_ref[...].astype(BF16), preferred_element_type=F32)
    up = jnp.dot(a, w3_ref[...].astype(BF16), preferred_element_type=F32)
    o_ref[...] = (_silu(gate) * up).astype(o_ref.dtype)


def swiglu_up(a, w1, w3, *, tm=2048, tn=256, name="swiglu_up"):
    m, k = a.shape
    n = w1.shape[1]
    assert m % tm == 0 and n % tn == 0
    w_spec = pl.BlockSpec((k, tn), lambda i, j: (0, j))
    return pl.pallas_call(
        _mm_rows_kernel,
        grid=(m // tm, n // tn),
        in_specs=[pl.BlockSpec((tm, k), lambda i, j: (i, 0), pipeline_mode=pl.Buffered(1)),
                  w_spec, w_spec],
        out_specs=pl.BlockSpec((tm, tn), lambda i, j: (i, j)),
        out_shape=jax.ShapeDtypeStruct((m, n), BF16),
        compiler_params=_params("parallel", "arbitrary"),
        name=name,
    )(a, w1, w3)


def _mm_ktiled_kernel(a_ref, w_ref, r_ref, o_ref):
    @pl.when(pl.program_id(2) == 0)
    def _():
        o_ref[...] = r_ref[...]

    o_ref[...] += jnp.dot(a_ref[...], w_ref[...].astype(BF16), preferred_element_type=F32)


def matmul_ktiled_res(a, w, res, *, tm=1024, tn=1024, tk=2048, name="matmul_ktiled"):
    m, k = a.shape
    n = w.shape[1]
    assert m % tm == 0 and n % tn == 0 and k % tk == 0
    return pl.pallas_call(
        _mm_ktiled_kernel,
        grid=(m // tm, n // tn, k // tk),
        in_specs=[pl.BlockSpec((tm, tk), lambda i, j, kk: (i, kk)),
                  pl.BlockSpec((tk, tn), lambda i, j, kk: (kk, j)),
                  pl.BlockSpec((tm, tn), lambda i, j, kk: (i, j))],
        out_specs=pl.BlockSpec((tm, tn), lambda i, j, kk: (i, j)),
        out_shape=jax.ShapeDtypeStruct((m, n), F32),
        compiler_params=_params("parallel", "parallel", "arbitrary"),
        name=name,
    )(a, w, res)


def _moba_kernel(q_ref, k_ref, v_ref, o_ref, kmean_ref, sel_ref, *, blk, nblk, topk, scale, span):
    qb = pl.program_id(2)
    nbp = kmean_ref.shape[0]

    @pl.when(qb == 0)
    def _():
        kmean_ref[...] = jnp.zeros_like(kmean_ref)
        for n in range(nblk):
            kb = k_ref[n * blk:(n + 1) * blk, :].astype(F32)
            kmean_ref[n:n + 1, :] = jnp.mean(kb, axis=0, keepdims=True)

    qs = q_ref[...].astype(F32) * scale
    gate = _nt(kmean_ref[...], qs, precision=HIGHEST)
    bid = lax.broadcasted_iota(jnp.int32, gate.shape, 0)
    t = jnp.where(bid < qb, gate, -jnp.inf)
    sel = jnp.zeros(gate.shape, F32)
    for _ in range(topk):
        mx = jnp.max(t, axis=0, keepdims=True)
        idx = jnp.min(jnp.where(t == mx, bid, nbp), axis=0, keepdims=True)
        pick = (bid == idx) & (mx > -jnp.inf)
        sel = jnp.where(pick, 1.0, sel)
        t = jnp.where(bid == idx, -jnp.inf, t)
    sel_ref[...] = jnp.concatenate([sel, jnp.zeros((LANES - nbp, blk), F32)], axis=0).T

    qsb = qs.astype(BF16)
    row = lax.broadcasted_iota(jnp.int32, (blk, blk), 0)
    col = lax.broadcasted_iota(jnp.int32, (blk, blk), 1)
    ahead = col - row
    for nv in range(span, nblk + 1, span):
        @pl.when((qb >= nv - span) & (qb < nv))
        def _(nv=nv):
            s = _nt(qsb, k_ref[0:nv * blk, :])
            pieces = []
            for kb in range(nv):
                sb = s[:, kb * blk:(kb + 1) * blk]
                if kb >= nv - span:
                    own_limit = jnp.where(qb == kb, 0, -2 * blk)
                    piece = jnp.where(ahead <= own_limit, sb, NEG)
                else:
                    piece = NEG
                pieces.append(jnp.where(sel_ref[:, kb:kb + 1] > 0.5, sb, piece))
            m = jnp.max(functools.reduce(jnp.maximum, pieces), axis=-1, keepdims=True)
            probs = [jnp.exp(piece - m) for piece in pieces]
            l = jnp.sum(functools.reduce(jnp.add, probs), axis=-1, keepdims=True)
            p = jnp.concatenate([pr.astype(BF16) for pr in probs], axis=1)
            o = jnp.dot(p, v_ref[0:nv * blk, :], preferred_element_type=F32)
            o_ref[...] = (o / l).astype(o_ref.dtype)


def moba_attention(qkv, *, nh, dh, span=2):
    bsz, t, _ = qkv.shape
    blk = MOBA_BLOCK
    nblk = t // blk
    nbp = -(-nblk // 8) * 8
    assert t % blk == 0 and nbp <= LANES and dh % LANES == 0 and nblk % span == 0
    kern = functools.partial(_moba_kernel, blk=blk, nblk=nblk, topk=min(MOBA_TOPK, nblk),
                             scale=dh ** -0.5, span=span)
    return pl.pallas_call(
        kern,
        grid=(bsz, nh, nblk),
        in_specs=[pl.BlockSpec((None, blk, dh), lambda b, h, i: (b, i, h)),
                  pl.BlockSpec((None, t, dh), lambda b, h, i: (b, 0, nh + h)),
                  pl.BlockSpec((None, t, dh), lambda b, h, i: (b, 0, 2 * nh + h))],
        out_specs=pl.BlockSpec((None, blk, dh), lambda b, h, i: (b, i, h)),
        out_shape=jax.ShapeDtypeStruct((bsz, t, nh * dh), BF16),
        scratch_shapes=[pltpu.VMEM((nbp, dh), F32),
                        pltpu.VMEM((blk, LANES), F32)],
        compiler_params=_params("parallel", "parallel", "arbitrary"),
        name="moba_attention",
    )(qkv, qkv, qkv)


def _causal_conv(x, buf_ref, w_ref, b_ref, first):
    rows = x.shape[0]

    @pl.when(first)
    def _():
        buf_ref[0:CONV_PAD, :] = jnp.zeros((CONV_PAD, x.shape[1]), F32)

    buf_ref[CONV_PAD:CONV_PAD + rows, :] = x
    y = b_ref[...] + w_ref[CONV_WIDTH - 1:CONV_WIDTH, :] * x
    for kk in range(CONV_WIDTH - 1):
        off = CONV_PAD - (CONV_WIDTH - 1) + kk
        y = y + w_ref[kk:kk + 1, :] * buf_ref[off:off + rows, :]
    buf_ref[0:CONV_PAD, :] = x[rows - CONV_PAD:rows, :]
    return y


def _lru_kernel(xr_ref, gr_ref, cw_ref, cb_ref, wa_ref, ba_ref, wx_ref, bx_ref, lam_ref, o_ref,
                xbuf_ref, h_ref, *, nblocks, bdim):
    ti = pl.program_id(2)
    rows = xr_ref.shape[0]

    @pl.when(ti == 0)
    def _():
        h_ref[...] = jnp.zeros_like(h_ref)

    x = _causal_conv(xr_ref[...], xbuf_ref, cw_ref, cb_ref, ti == 0)
    ga, gx = [], []
    for n in range(nblocks):
        xb = x[:, n * bdim:(n + 1) * bdim].astype(BF16)
        ga.append(jnp.dot(xb, wa_ref[n].astype(BF16), preferred_element_type=F32))
        gx.append(jnp.dot(xb, wx_ref[n].astype(BF16), preferred_element_type=F32))
    gate_a = jnp.concatenate(ga, axis=-1) + ba_ref[...]
    gate_x = jnp.concatenate(gx, axis=-1) + bx_ref[...]
    log_a = -LRU_C * _sigmoid(gate_a) * _softplus(-lam_ref[...])
    a = jnp.exp(log_a)
    u = jnp.sqrt(1.0 - jnp.exp(2.0 * log_a)) * _sigmoid(gate_x) * x

    sub = jnp.bitwise_and(lax.broadcasted_iota(jnp.int32, a.shape, 0), SUBLANES - 1)
    s = 1
    while s < SUBLANES:
        keep = sub >= s
        a_prev = jnp.where(keep, pltpu.roll(a, s, 0), 1.0)
        u_prev = jnp.where(keep, pltpu.roll(u, s, 0), 0.0)
        u = a * u_prev + u
        a = a * a_prev
        s *= 2
    carry = h_ref[0:1, :]
    groups = []
    for gi in range(rows // SUBLANES):
        lo = gi * SUBLANES
        hg = u[lo:lo + SUBLANES] + a[lo:lo + SUBLANES] * carry
        groups.append(hg)
        carry = hg[SUBLANES - 1:SUBLANES, :]
    h = jnp.concatenate(groups, axis=0)
    h_ref[0:1, :] = carry

    g = gr_ref[...]
    gelu = 0.5 * g * (1.0 + jnp.tanh(math.sqrt(2.0 / math.pi) * (g + 0.044715 * g * g * g)))
    o_ref[...] = (h * gelu).astype(o_ref.dtype)


def conv_rglru(xg, conv_w, conv_b, w_a, b_a, w_x, b_x, lam, *, tt=256, tc=512):
    bsz, t, w2 = xg.shape
    w = w2 // 2
    nblocks_all, bdim, _ = w_a.shape
    nb = tc // bdim
    assert t % tt == 0 and w % tc == 0 and tc % bdim == 0
    row = lambda v: v.reshape(1, w)
    vec_spec = pl.BlockSpec((1, tc), lambda b, c, i: (0, c))
    gate_spec = pl.BlockSpec((nb, bdim, bdim), lambda b, c, i: (c, 0, 0))
    kern = functools.partial(_lru_kernel, nblocks=nb, bdim=bdim)
    return pl.pallas_call(
        kern,
        grid=(bsz, w // tc, t // tt),
        in_specs=[pl.BlockSpec((None, tt, tc), lambda b, c, i: (b, i, c)),
                  pl.BlockSpec((None, tt, tc), lambda b, c, i: (b, i, w // tc + c)),
                  pl.BlockSpec((CONV_WIDTH, tc), lambda b, c, i: (0, c)),
                  vec_spec, gate_spec, vec_spec, gate_spec, vec_spec, vec_spec],
        out_specs=pl.BlockSpec((None, tt, tc), lambda b, c, i: (b, i, c)),
        out_shape=jax.ShapeDtypeStruct((bsz, t, w), BF16),
        scratch_shapes=[pltpu.VMEM((CONV_PAD + tt, tc), F32),
                        pltpu.VMEM((8, tc), F32)],
        compiler_params=_params("parallel", "parallel", "arbitrary"),
        name="conv_rglru",
    )(xg, xg, conv_w, row(conv_b), w_a, row(b_a), w_x, row(b_x), row(lam))


def _gla_kernel(q_ref, k_ref, v_ref, r_ref, lr_ref, wa_ref, ba_ref, g_ref, o_ref, st_ref, *,
                chunk, scale, hps, dk, dv):
    ci = pl.program_id(2)

    @pl.when(ci == 0)
    def _():
        st_ref[...] = jnp.zeros_like(st_ref)

    z = jnp.dot(lr_ref[...].astype(BF16), wa_ref[...].astype(BF16),
                preferred_element_type=F32) + ba_ref[...]
    g = (jnp.minimum(z, 0.0) - jnp.log(1.0 + jnp.exp(-jnp.abs(z)))) / GLA_TAU
    row = lax.broadcasted_iota(jnp.int32, (chunk, chunk), 0)
    col = lax.broadcasted_iota(jnp.int32, (chunk, chunk), 1)
    sub0 = row - jnp.bitwise_and(row, GLA_SUB - 1)
    sums = jnp.concatenate([jnp.where(col <= row, 1.0, 0.0), jnp.where(col < sub0, 1.0, 0.0)],
                           axis=0).astype(BF16)
    bb = _dot_exact01(sums, g, ones_on_left=True)
    jrow = lax.broadcasted_iota(jnp.int32, (chunk, dk), 0)
    srow = lax.broadcasted_iota(jnp.int32, (GLA_SUB, chunk), 0)
    scol = lax.broadcasted_iota(jnp.int32, (GLA_SUB, chunk), 1)

    for h in range(hps):
        b = bb[0:chunk, h * dk:(h + 1) * dk]
        b_sub = bb[chunk:2 * chunk, h * dk:(h + 1) * dk]
        q = q_ref[:, h * dk:(h + 1) * dk].astype(F32) * scale
        k = k_ref[:, h * dk:(h + 1) * dk].astype(F32)
        v = v_ref[:, h * dv:(h + 1) * dv]
        b_last = b[chunk - 1:chunk, :]

        o = _nt((q * jnp.exp(b)).astype(BF16), st_ref[h].astype(BF16))

        parts = []
        for i in range(chunk // GLA_SUB):
            lo, hi = i * GLA_SUB, (i + 1) * GLA_SUB
            ref = b_sub[lo:lo + 1, :]
            qi = (q[lo:hi] * jnp.exp(b[lo:hi] - ref)).astype(BF16)
            ki = (k * jnp.exp(jnp.where(jrow < hi, ref - b, 0.0))).astype(BF16)
            att = _nt(qi, ki)
            att = jnp.where(scol <= srow + lo, att, 0.0)
            parts.append(jnp.dot(att.astype(BF16), v, preferred_element_type=F32))
        o = o + jnp.concatenate(parts, axis=0)

        kd = (k * jnp.exp(b_last - b)).astype(BF16)
        upd = lax.dot_general(v, kd, (((0,), (0,)), ((), ())), preferred_element_type=F32)
        st_ref[h] = st_ref[h] * jnp.exp(b_last) + upd

        y = o * lax.rsqrt(jnp.mean(o * o, axis=-1, keepdims=True) + NORM_EPS) * g_ref[...]
        gate = _silu(r_ref[:, h * dv:(h + 1) * dv].astype(F32))
        o_ref[:, h * dv:(h + 1) * dv] = (y * gate).astype(o_ref.dtype)


def gla(p1a, sm, w_a2p, b_a, norm, *, nh, dk, dv, chunk=128, hps=4):
    bsz, t, _ = p1a.shape
    hps = min(hps, nh)
    assert t % chunk == 0 and chunk % GLA_SUB == 0 and nh % hps == 0
    kq = nh * dk
    gk, gv = hps * dk, hps * dv
    koff, voff, roff = kq // gk, 2 * kq // gv, (2 * kq + nh * dv) // gv
    kern = functools.partial(_gla_kernel, chunk=chunk, scale=dk ** -0.5, hps=hps, dk=dk, dv=dv)
    return pl.pallas_call(
        kern,
        grid=(bsz, nh // hps, t // chunk),
        in_specs=[pl.BlockSpec((None, chunk, gk), lambda b, h, c: (b, c, h)),
                  pl.BlockSpec((None, chunk, gk), lambda b, h, c: (b, c, koff + h)),
                  pl.BlockSpec((None, chunk, gv), lambda b, h, c: (b, c, voff + h)),
                  pl.BlockSpec((None, chunk, gv), lambda b, h, c: (b, c, roff + h)),
                  pl.BlockSpec((None, chunk, LANES), lambda b, h, c: (b, c, 0)),
                  pl.BlockSpec((LANES, gk), lambda b, h, c: (0, h)),
                  pl.BlockSpec((1, gk), lambda b, h, c: (0, h)),
                  pl.BlockSpec((1, dv), lambda b, h, c: (0, 0))],
        out_specs=pl.BlockSpec((None, chunk, gv), lambda b, h, c: (b, c, h)),
        out_shape=jax.ShapeDtypeStruct((bsz, t, nh * dv), BF16),
        scratch_shapes=[pltpu.VMEM((hps, dv, dk), F32)],
        compiler_params=_params("parallel", "parallel", "arbitrary"),
        name="gla",
    )(p1a, p1a, p1a, p1a, sm, w_a2p, b_a.reshape(1, kq), norm.reshape(1, dv))


def _ssd_kernel(z_ref, xs_ref, bm_ref, cm_ref, dt_ref, cwx_ref, cwb_ref, cwc_ref, cbx_ref, cbb_ref,
                cbc_ref, dtb_ref, alog_ref, dfull_ref, norm_ref, o_ref,
                xbuf_ref, bbuf_ref, cbuf_ref, st_ref, *, chunk, hpg, hd):
    ci = pl.program_id(2)
    first = ci == 0

    @pl.when(first)
    def _():
        st_ref[...] = jnp.zeros_like(st_ref)

    xs = _silu(_causal_conv(xs_ref[...], xbuf_ref, cwx_ref, cbx_ref, first))
    bm = _silu(_causal_conv(bm_ref[...], bbuf_ref, cwb_ref, cbb_ref, first))
    cm = _silu(_causal_conv(cm_ref[...], cbuf_ref, cwc_ref, cbc_ref, first))
    width = hpg * hd

    dt = _softplus(dt_ref[...] + dtb_ref[...])
    da = dt * (-jnp.exp(alog_ref[...]))
    row = lax.broadcasted_iota(jnp.int32, (chunk, chunk), 0)
    col = lax.broadcasted_iota(jnp.int32, (chunk, chunk), 1)
    causal = col <= row
    tril = jnp.where(causal, 1.0, 0.0).astype(BF16)
    cs = _dot_exact01(tril, da, ones_on_left=True)
    cs_t = cs.T

    erow = lax.broadcasted_iota(jnp.int32, (LANES, width), 0)
    ecol = lax.broadcasted_iota(jnp.int32, (LANES, width), 1)
    expand = jnp.where((ecol >= erow * hd) & (ecol < (erow + 1) * hd), 1.0, 0.0).astype(BF16)
    full = _dot_exact01(expand, jnp.concatenate([cs, dt], axis=0), ones_on_left=False)
    cs_full = full[0:chunk]
    dt_full = full[chunk:2 * chunk]
    cs_last = cs_full[chunk - 1:chunk, :]

    xd = xs * dt_full
    xd_b = xd.astype(BF16)
    bm_b = bm.astype(BF16)
    cm_b = cm.astype(BF16)
    cb = _nt(cm_b, bm_b)

    lane = lax.broadcasted_iota(jnp.int32, (chunk, 2 * hd), 1)
    slabs = []
    for j in range(hpg // 2):
        xpair = xd_b[:, 2 * j * hd:(2 * j + 2) * hd]
        acc = None
        for half in range(2):
            h = 2 * j + half
            seg = cs[:, h:h + 1] - cs_t[h:h + 1, :]
            lmat = jnp.exp(jnp.where(causal, seg, NEG))
            mh = (cb * lmat).astype(BF16)
            mine = (lane >= half * hd) & (lane < (half + 1) * hd)
            part = jnp.dot(mh, jnp.where(mine, xpair, jnp.zeros_like(xpair)),
                           preferred_element_type=F32)
            acc = part if acc is None else acc + part
        slabs.append(acc)
    y = jnp.concatenate(slabs, axis=-1)

    st = st_ref[...]
    y = y + jnp.dot(cm_b, st.astype(BF16), preferred_element_type=F32) * jnp.exp(cs_full)
    xdd = (xd * jnp.exp(cs_last - cs_full)).astype(BF16)
    upd = lax.dot_general(bm_b, xdd, (((0,), (0,)), ((), ())), preferred_element_type=F32)
    st_ref[...] = st * jnp.exp(cs_last) + upd

    y = y + dfull_ref[...] * xs
    y = y * _silu(z_ref[...])
    y = y * lax.rsqrt(jnp.mean(y * y, axis=-1, keepdims=True) + NORM_EPS) * norm_ref[...]
    o_ref[...] = y.astype(o_ref.dtype)


def ssd(p1b, sm, conv_w, conv_b, dt_bias, a_log, d_skip, norm, *, nheads, hd, groups, nstate,
        chunk=256):
    bsz, t, _ = p1b.shape
    width = nheads * hd
    gw = width // groups
    hpg = nheads // groups
    assert t % chunk == 0 and nstate == LANES and gw % LANES == 0 and hpg % 2 == 0
    xoff = width // gw
    boff = 2 * width // nstate
    coff = boff + groups
    cxo, cbo, cco = 0, width // nstate, width // nstate + groups
    padl = lambda v: jnp.pad(v.reshape(groups, 1, hpg), ((0, 0), (0, 0), (0, LANES - hpg)))
    dfull = jnp.repeat(d_skip.reshape(groups, hpg), hd, axis=1).reshape(groups, 1, gw)
    cb2 = conv_b.reshape(1, -1)
    g3 = lambda b, g, c: (g, 0, 0)
    kern = functools.partial(_ssd_kernel, chunk=chunk, hpg=hpg, hd=hd)
    return pl.pallas_call(
        kern,
        grid=(bsz, groups, t // chunk),
        in_specs=[pl.BlockSpec((None, chunk, gw), lambda b, g, c: (b, c, g)),
                  pl.BlockSpec((None, chunk, gw), lambda b, g, c: (b, c, xoff + g)),
                  pl.BlockSpec((None, chunk, nstate), lambda b, g, c: (b, c, boff + g)),
                  pl.BlockSpec((None, chunk, nstate), lambda b, g, c: (b, c, coff + g)),
                  pl.BlockSpec((None, chunk, LANES), lambda b, g, c: (b, c, 1 + g)),
                  pl.BlockSpec((CONV_WIDTH, gw), lambda b, g, c: (0, cxo + g)),
                  pl.BlockSpec((CONV_WIDTH, nstate), lambda b, g, c: (0, cbo + g)),
                  pl.BlockSpec((CONV_WIDTH, nstate), lambda b, g, c: (0, cco + g)),
                  pl.BlockSpec((1, gw), lambda b, g, c: (0, cxo + g)),
                  pl.BlockSpec((1, nstate), lambda b, g, c: (0, cbo + g)),
                  pl.BlockSpec((1, nstate), lambda b, g, c: (0, cco + g)),
                  pl.BlockSpec((None, 1, LANES), g3),
                  pl.BlockSpec((None, 1, LANES), g3),
                  pl.BlockSpec((None, 1, gw), g3),
                  pl.BlockSpec((1, gw), lambda b, g, c: (0, g))],
        out_specs=pl.BlockSpec((None, chunk, gw), lambda b, g, c: (b, c, g)),
        out_shape=jax.ShapeDtypeStruct((bsz, t, width), BF16),
        scratch_shapes=[pltpu.VMEM((CONV_PAD + chunk, gw), F32),
                        pltpu.VMEM((CONV_PAD + chunk, nstate), F32),
                        pltpu.VMEM((CONV_PAD + chunk, nstate), F32),
                        pltpu.VMEM((nstate, gw), F32)],
        compiler_params=_params("parallel", "parallel", "arbitrary"),
        name="ssd",
    )(p1b, p1b, p1b, p1b, sm, conv_w, conv_w, conv_w, cb2, cb2, cb2,
      padl(dt_bias), padl(a_log), dfull, norm.reshape(1, width))


def _norm_router_kernel(x_ref, g_ref, r_ref, h_ref, route_ref, *, n_exp):
    x = x_ref[...]
    y = x * lax.rsqrt(jnp.mean(x * x, axis=-1, keepdims=True) + NORM_EPS) * g_ref[...]
    h_ref[...] = y
    logits = jnp.dot(y, r_ref[...], precision=HIGHEST, preferred_element_type=F32)
    lane = lax.broadcasted_iota(jnp.int32, logits.shape, 1)
    t = jnp.where(lane < n_exp, logits, -jnp.inf)
    m1 = jnp.max(t, axis=-1, keepdims=True)
    i1 = jnp.min(jnp.where(t == m1, lane, LANES), axis=-1, keepdims=True)
    t2 = jnp.where(lane == i1, -jnp.inf, t)
    m2 = jnp.max(t2, axis=-1, keepdims=True)
    i2 = jnp.min(jnp.where(t2 == m2, lane, LANES), axis=-1, keepdims=True)
    e = jnp.exp(m2 - m1)
    w1 = 1.0 / (1.0 + e)
    w2 = e / (1.0 + e)
    route = jnp.where(lane == 0, i1.astype(F32),
                      jnp.where(lane == 1, i2.astype(F32),
                                jnp.where(lane == 2, w1, jnp.where(lane == 3, w2, 0.0))))
    route_ref[...] = route


def norm_router(x, g, router, tm=256):
    m, d = x.shape
    n_exp = router.shape[1]
    rp = jnp.pad(router, ((0, 0), (0, LANES - n_exp)))
    return pl.pallas_call(
        functools.partial(_norm_router_kernel, n_exp=n_exp),
        grid=(m // tm,),
        in_specs=[pl.BlockSpec((tm, d), lambda i: (i, 0)),
                  pl.BlockSpec((1, d), lambda i: (0, 0)),
                  pl.BlockSpec((d, LANES), lambda i: (0, 0))],
        out_specs=[pl.BlockSpec((tm, d), lambda i: (i, 0)),
                   pl.BlockSpec((tm, LANES), lambda i: (i, 0))],
        out_shape=[jax.ShapeDtypeStruct((m, d), F32), jax.ShapeDtypeStruct((m, LANES), F32)],
        compiler_params=_params("parallel"),
        name="norm_router",
    )(x, g.reshape(1, d), rp)


def _row_gather(idx_ref, base, src_hbm, dst_ref, sem, rows):
    def start(i, c):
        for j in range(GATHER_UNROLL):
            r = i * GATHER_UNROLL + j
            pltpu.make_async_copy(src_hbm.at[pl.ds(idx_ref[base + r], 1)],
                                  dst_ref.at[pl.ds(r, 1)], sem).start(priority=j % 2)
        return c

    lax.fori_loop(0, rows // GATHER_UNROLL, start, 0)
    pltpu.make_async_copy(src_hbm.at[pl.ds(0, rows)], dst_ref, sem).wait()


def _dispatch_kernel(tok_ref, live_ref, h_hbm, o_ref, buf_ref, sem, *, sub):
    i = pl.program_id(0)
    rows = buf_ref.shape[0]
    live = live_ref[i // sub] - (i % sub) * rows

    @pl.when(live > 0)
    def _():
        _row_gather(tok_ref, i * rows, h_hbm, buf_ref, sem, rows)
        o_ref[...] = buf_ref[...].astype(o_ref.dtype)

    @pl.when(live <= 0)
    def _():
        o_ref[...] = jnp.zeros_like(o_ref)


def moe_dispatch(row_token, tile_rows, h, *, tile, tm=256):
    r = row_token.shape[0]
    d = h.shape[1]
    assert tm % GATHER_UNROLL == 0 and tile % tm == 0 and r % tile == 0
    return pl.pallas_call(
        functools.partial(_dispatch_kernel, sub=tile // tm),
        grid_spec=pltpu.PrefetchScalarGridSpec(
            num_scalar_prefetch=2,
            grid=(r // tm,),
            in_specs=[pl.BlockSpec(memory_space=pl.ANY)],
            out_specs=pl.BlockSpec((tm, d), lambda i, tok, live: (i, 0)),
            scratch_shapes=[pltpu.VMEM((tm, d), F32), pltpu.SemaphoreType.DMA(())]),
        out_shape=jax.ShapeDtypeStruct((r, d), BF16),
        compiler_params=_params("arbitrary"),
        name="moe_dispatch",
    )(row_token, tile_rows, h)


def _moe_mm_kernel(te_ref, tv_ref, x_ref, *refs, act_pair):
    n_w = 2 if act_pair else 1
    w_refs = refs[:n_w]
    o_ref = refs[n_w]
    r = pl.program_id(1)

    live = tv_ref[r]
    tm = x_ref.shape[0]
    half = tm // 2

    def compute(nrows):
        x = x_ref[0:nrows, :]
        out = jnp.dot(x, w_refs[0][...].astype(BF16), preferred_element_type=F32)
        if act_pair:
            out = _silu(out) * jnp.dot(x, w_refs[1][...].astype(BF16), preferred_element_type=F32)
        o_ref[0:nrows, :] = out.astype(o_ref.dtype)

    @pl.when(live > half)
    def _():
        compute(tm)

    @pl.when((live > 0) & (live <= half))
    def _():
        compute(half)
        o_ref[half:tm, :] = jnp.zeros((tm - half, o_ref.shape[1]), o_ref.dtype)

    @pl.when(live == 0)
    def _():
        o_ref[...] = jnp.zeros_like(o_ref)


def moe_matmul(tile_expert, tile_rows, x, w_list, *, out_dtype, tm=MOE_ROW_TILE, tn=256,
               name="moe_matmul"):
    r, k = x.shape
    n = w_list[0].shape[2]
    assert r % tm == 0 and n % tn == 0
    kern = functools.partial(_moe_mm_kernel, act_pair=len(w_list) == 2)
    w_spec = pl.BlockSpec((None, k, tn), lambda j, i, te, tv: (te[i], 0, j))
    return pl.pallas_call(
        kern,
        grid_spec=pltpu.PrefetchScalarGridSpec(
            num_scalar_prefetch=2,
            grid=(n // tn, r // tm),
            in_specs=[pl.BlockSpec((tm, k), lambda j, i, te, tv: (i, 0))] + [w_spec] * len(w_list),
            out_specs=pl.BlockSpec((tm, tn), lambda j, i, te, tv: (i, j))),
        out_shape=jax.ShapeDtypeStruct((r, n), out_dtype),
        compiler_params=_params("parallel", "arbitrary"),
        name=name,
    )(tile_expert, tile_rows, x, *w_list)


def _combine_kernel(d0_ref, d1_ref, x_ref, route_ref, g_ref, y_hbm, o_ref, buf0_ref, buf1_ref, sems):
    rows = x_ref.shape[0]
    base = pl.program_id(0) * rows
    _row_gather(d0_ref, base, y_hbm, buf0_ref, sems.at[0], rows)
    _row_gather(d1_ref, base, y_hbm, buf1_ref, sems.at[1], rows)
    w0 = route_ref[:, 2:3]
    w1 = route_ref[:, 3:4]
    x = x_ref[...] + w0 * buf0_ref[...] + w1 * buf1_ref[...]
    y = x * lax.rsqrt(jnp.mean(x * x, axis=-1, keepdims=True) + NORM_EPS)
    o_ref[...] = y * g_ref[...]


def moe_combine_norm(dest0, dest1, x, route, y, g, tm=256):
    m, d = x.shape
    assert tm % GATHER_UNROLL == 0
    return pl.pallas_call(
        _combine_kernel,
        grid_spec=pltpu.PrefetchScalarGridSpec(
            num_scalar_prefetch=2,
            grid=(m // tm,),
            in_specs=[pl.BlockSpec((tm, d), lambda i, a, b: (i, 0)),
                      pl.BlockSpec((tm, LANES), lambda i, a, b: (i, 0)),
                      pl.BlockSpec((1, d), lambda i, a, b: (0, 0)),
                      pl.BlockSpec(memory_space=pl.ANY)],
            out_specs=pl.BlockSpec((tm, d), lambda i, a, b: (i, 0)),
            scratch_shapes=[pltpu.VMEM((tm, d), F32), pltpu.VMEM((tm, d), F32),
                            pltpu.SemaphoreType.DMA((2,))]),
        out_shape=jax.ShapeDtypeStruct((m, d), F32),
        compiler_params=_params("arbitrary"),
        name="moe_combine_norm",
    )(dest0, dest1, x, route, g.reshape(1, d), y)


def _moe_plan(ids, tm):
    n_tok = ids.shape[0]
    e_flat = ids.reshape(-1)
    onehot = (e_flat[:, None] == jnp.arange(N_EXPERTS, dtype=jnp.int32)[None, :]).astype(jnp.int32)
    csum = jnp.cumsum(onehot, axis=0)
    pos = jnp.sum(csum * onehot, axis=1) - 1
    counts = csum[-1]
    padded = ((counts + tm - 1) // tm) * tm
    gend = jnp.cumsum(padded)
    gstart = gend - padded
    dest = gstart[e_flat] + pos
    n_rows = 2 * n_tok + N_EXPERTS * tm
    row_token = jnp.zeros((n_rows,), jnp.int32).at[dest].set(
        jnp.arange(2 * n_tok, dtype=jnp.int32) // 2)
    tile_start = jnp.arange(n_rows // tm, dtype=jnp.int32) * tm
    last_start = jnp.maximum(gend[-1] - tm, 0)
    probe = jnp.minimum(tile_start, last_start)
    tile_expert = jnp.minimum(jnp.sum((probe[:, None] >= gend[None, :]).astype(jnp.int32), axis=1),
                              N_EXPERTS - 1).astype(jnp.int32)
    live_end = (gstart + counts)[tile_expert]
    tile_rows = jnp.where(tile_start < gend[-1], jnp.clip(live_end - tile_start, 0, tm), 0)
    dest2 = dest.reshape(n_tok, 2).astype(jnp.int32)
    return row_token, tile_expert, tile_rows.astype(jnp.int32), dest2[:, 0], dest2[:, 1]


def kernel(x, ev_norm1, ev_w_in, ev_lru_conv_w, ev_lru_conv_b, ev_lru_w_a, ev_lru_b_a, ev_lru_w_x, ev_lru_b_x, ev_lru_lambda, ev_w_out, ev_norm2, ev_ffn_w1, ev_ffn_w3, ev_ffn_w2, od_norm1, od_w_in, od_gla_w_a2, od_gla_b_a, od_gla_norm, od_ssd_conv_w, od_ssd_conv_b, od_ssd_dt_bias, od_ssd_a_log, od_ssd_d, od_ssd_norm, od_w_out, od_norm2, od_router, od_moe_w1, od_moe_w3, od_moe_w2, final_norm):
    bsz, t, d = x.shape
    n_tok = bsz * t
    xf = x.reshape(n_tok, d)

    lru_w = ev_lru_lambda.shape[1]
    moba_w = (ev_w_in.shape[2] - 2 * lru_w) // 3
    moba_dh = LANES
    h = rmsnorm(xf, ev_norm1[0])
    qkv = matmul([h], [ev_w_in[0]], w_col0=0, n_cols=3 * moba_w, out_dtype=BF16, name="l0_in_qkv")
    xg = matmul([h], [ev_w_in[0]], w_col0=3 * moba_w, n_cols=2 * lru_w, out_dtype=F32, name="l0_in_lru")
    att = moba_attention(qkv.reshape(bsz, t, 3 * moba_w), nh=moba_w // moba_dh, dh=moba_dh)
    rec = conv_rglru(xg.reshape(bsz, t, 2 * lru_w), ev_lru_conv_w[0], ev_lru_conv_b[0], ev_lru_w_a[0],
                     ev_lru_b_a[0], ev_lru_w_x[0], ev_lru_b_x[0], ev_lru_lambda[0])
    xf = matmul([att.reshape(n_tok, moba_w), rec.reshape(n_tok, lru_w)], [ev_w_out[0]], res=xf,
                name="l0_out")
    h = rmsnorm(xf, ev_norm2[0])
    gact = swiglu_up(h, ev_ffn_w1[0], ev_ffn_w3[0], name="l0_ffn_up")
    xf = matmul_ktiled_res(gact, ev_ffn_w2[0], xf, name="l0_ffn_down")

    gla_key = od_gla_w_a2.shape[2]
    gla_rank = od_gla_w_a2.shape[1]
    gla_dv = od_gla_norm.shape[1]
    gla_heads = gla_key // (gla_dv // 2)
    gla_val = gla_heads * gla_dv
    ssd_heads = od_ssd_a_log.shape[1]
    ssd_width = od_ssd_norm.shape[1]
    ssd_conv_dim = od_ssd_conv_w.shape[2]
    ssd_groups = 4
    ssd_state = (ssd_conv_dim - ssd_width) // (2 * ssd_groups)
    hpg = ssd_heads // ssd_groups
    w_in1 = od_w_in[0]
    c_lr = 2 * gla_key + 2 * gla_val
    c_z = c_lr + gla_rank
    c_dt = c_z + ssd_width + ssd_conv_dim
    zpad = lambda n: jnp.zeros((d, n), F32)
    w_small = jnp.concatenate(
        [w_in1[:, c_lr:c_z], zpad(LANES - gla_rank)]
        + [blk for g in range(ssd_groups)
           for blk in (w_in1[:, c_dt + g * hpg:c_dt + (g + 1) * hpg], zpad(LANES - hpg))], axis=1)

    h = rmsnorm(xf, od_norm1[0])
    p1a = matmul([h], [w_in1], w_col0=0, n_cols=c_lr, out_dtype=BF16, name="l1_in_gla")
    p1b = matmul([h], [w_in1], w_col0=c_z, n_cols=c_dt - c_z, out_dtype=F32, name="l1_in_ssd")
    sm = matmul([h], [w_small], out_dtype=F32, tn=w_small.shape[1], name="l1_in_small")
    w_a2p = jnp.pad(od_gla_w_a2[0], ((0, LANES - gla_rank), (0, 0)))
    o_gla = gla(p1a.reshape(bsz, t, c_lr), sm.reshape(bsz, t, -1), w_a2p, od_gla_b_a[0], od_gla_norm[0],
                nh=gla_heads, dk=gla_key // gla_heads, dv=gla_dv)
    y_ssd = ssd(p1b.reshape(bsz, t, -1), sm.reshape(bsz, t, -1), od_ssd_conv_w[0], od_ssd_conv_b[0],
                od_ssd_dt_bias[0], od_ssd_a_log[0], od_ssd_d[0], od_ssd_norm[0],
                nheads=ssd_heads, hd=ssd_width // ssd_heads, groups=ssd_groups, nstate=ssd_state)
    xf = matmul([o_gla.reshape(n_tok, gla_val), y_ssd.reshape(n_tok, ssd_width)], [od_w_out[0]], res=xf,
                name="l1_out")

    hf, route = norm_router(xf, od_norm2[0], od_router[0])
    ids = route[:, 0:2].astype(jnp.int32)
    row_token, tile_expert, tile_rows, dest0, dest1 = _moe_plan(ids, MOE_ROW_TILE)
    xs = moe_dispatch(row_token, tile_rows, hf, tile=MOE_ROW_TILE)
    gact = moe_matmul(tile_expert, tile_rows, xs, [od_moe_w1[0], od_moe_w3[0]], out_dtype=BF16,
                      tn=MOE_COL_TILE, name="moe_up")
    ye = moe_matmul(tile_expert, tile_rows, gact, [od_moe_w2[0]], out_dtype=F32,
                    tn=MOE_COL_TILE, name="moe_down")
    out = moe_combine_norm(dest0, dest1, xf, route, ye, final_norm)
    return out.reshape(bsz, t, d)
```

```python
import functools
import math

import jax
import jax.numpy as jnp
from jax import lax
from jax.experimental import pallas as pl
from jax.experimental.pallas import tpu as pltpu

F32 = jnp.float32
BF16 = jnp.bfloat16
HIGHEST = lax.Precision.HIGHEST

NORM_EPS = 1e-6
NEG = -1e30
LANES = 128
SUBLANES = 8
VMEM_LIMIT = 56 * 1024 * 1024

CONV_WIDTH = 4
CONV_PAD = 8
MOBA_BLOCK = 256
MOBA_TOPK = 3
LRU_C = 8.0
GLA_TAU = 16.0
GLA_SUB = 16
N_EXPERTS = 8
MOE_ROW_TILE = 512
MOE_COL_TILE = 512
GATHER_UNROLL = 8


def _nt(a, b, precision=None):
    return lax.dot_general(a, b, (((1,), (1,)), ((), ())), precision=precision,
                           preferred_element_type=F32)


def _dot_exact01(ones, x, *, ones_on_left):
    hi = x.astype(BF16)
    r1 = x - hi.astype(F32)
    mid = r1.astype(BF16)
    lo = (r1 - mid.astype(F32)).astype(BF16)
    out = None
    for term in (hi, mid, lo):
        part = (jnp.dot(ones, term, preferred_element_type=F32) if ones_on_left
                else jnp.dot(term, ones, preferred_element_type=F32))
        out = part if out is None else out + part
    return out


def _sigmoid(x):
    return 1.0 / (1.0 + jnp.exp(-x))


def _silu(x):
    return x * _sigmoid(x)


def _softplus(x):
    return jnp.maximum(x, 0.0) + jnp.log(1.0 + jnp.exp(-jnp.abs(x)))


def _params(*sem):
    return pltpu.CompilerParams(dimension_semantics=sem, vmem_limit_bytes=VMEM_LIMIT)


def _rmsnorm_kernel(x_ref, g_ref, o_ref):
    x = x_ref[...]
    y = x * lax.rsqrt(jnp.mean(x * x, axis=-1, keepdims=True) + NORM_EPS)
    o_ref[...] = (y * g_ref[...]).astype(o_ref.dtype)


def rmsnorm(x, g, out_dtype=BF16, tm=256):
    m, d = x.shape
    return pl.pallas_call(
        _rmsnorm_kernel,
        grid=(m // tm,),
        in_specs=[pl.BlockSpec((tm, d), lambda i: (i, 0)),
                  pl.BlockSpec((1, d), lambda i: (0, 0))],
        out_specs=pl.BlockSpec((tm, d), lambda i: (i, 0)),
        out_shape=jax.ShapeDtypeStruct((m, d), out_dtype),
        compiler_params=_params("parallel"),
        name="rmsnorm",
    )(x, g.reshape(1, d))


def _mm_kernel(*refs, n_a, has_res, act_pair):
    a_refs = refs[:n_a]
    n_w = 2 if act_pair else 1
    w_refs = refs[n_a:n_a + n_w]
    pos = n_a + n_w
    r_ref = refs[pos] if has_res else None
    pos += int(has_res)
    o_ref = refs[pos]
    wb_refs = refs[pos + 1:pos + 1 + n_w]

    @pl.when(pl.program_id(1) == 0)
    def _():
        for w_ref, wb_ref in zip(w_refs, wb_refs):
            wb_ref[...] = w_ref[...].astype(BF16)

    def contract(wb_ref):
        acc = None
        k0 = 0
        for a_ref in a_refs:
            ka = a_ref.shape[1]
            part = jnp.dot(a_ref[...], wb_ref[k0:k0 + ka, :], preferred_element_type=F32)
            acc = part if acc is None else acc + part
            k0 += ka
        return acc

    if act_pair:
        out = _silu(contract(wb_refs[0])) * contract(wb_refs[1])
    else:
        out = contract(wb_refs[0])
    if has_res:
        out = out + r_ref[...]
    o_ref[...] = out.astype(o_ref.dtype)


def matmul(a_list, w_list, *, w_col0=0, n_cols=None, res=None, out_dtype=F32, tm=1024, tn=512,
           name="matmul"):
    m = a_list[0].shape[0]
    k = sum(a.shape[1] for a in a_list)
    assert all(w.shape[0] == k for w in w_list)
    n_cols = w_list[0].shape[1] - w_col0 if n_cols is None else n_cols
    assert m % tm == 0 and n_cols % tn == 0 and w_col0 % tn == 0
    nb0 = w_col0 // tn
    in_specs = [pl.BlockSpec((tm, a.shape[1]), lambda n, i: (i, 0)) for a in a_list]
    in_specs += [pl.BlockSpec((k, tn), lambda n, i: (0, n + nb0)) for _ in w_list]
    args = list(a_list) + list(w_list)
    if res is not None:
        in_specs.append(pl.BlockSpec((tm, tn), lambda n, i: (i, n)))
        args.append(res)
    kern = functools.partial(_mm_kernel, n_a=len(a_list), has_res=res is not None,
                             act_pair=len(w_list) == 2)
    return pl.pallas_call(
        kern,
        grid=(n_cols // tn, m // tm),
        in_specs=in_specs,
        out_specs=pl.BlockSpec((tm, tn), lambda n, i: (i, n)),
        out_shape=jax.ShapeDtypeStruct((m, n_cols), out_dtype),
        scratch_shapes=[pltpu.VMEM((k, tn), BF16) for _ in w_list],
        compiler_params=_params("parallel", "arbitrary"),
        name=name,
    )(*args)


def _mm_wt_kernel(a_ref, w_ref, *rest, row_shift):
    if row_shift:
        wn_ref, o_ref, wb_ref = rest
    else:
        o_ref, wb_ref = rest

    @pl.when(pl.program_id(1) == 0)
    def _():
        tn = w_ref.shape[0]
        if row_shift:
            wb_ref[0:tn - row_shift, :] = w_ref[row_shift:tn, :].astype(BF16)
            wb_ref[tn - row_shift:tn, :] = wn_ref[...].astype(BF16)
        else:
            wb_ref[...] = w_ref[...].astype(BF16)

    o_ref[...] = _nt(a_ref[...], wb_ref[...]).astype(o_ref.dtype)


def matmul_wt(a, wt, *, row0=0, n_cols=None, out_dtype=F32, tm=1024, tn=512, name="matmul_wt"):
    m, k = a.shape
    assert wt.shape[1] == k
    n_cols = wt.shape[0] - row0 if n_cols is None else n_cols
    row_shift = row0 % tn
    nb0 = (row0 - row_shift) // tn
    assert m % tm == 0 and n_cols % tn == 0 and row_shift % 16 == 0 and (not row_shift or tn % row_shift == 0)
    in_specs = [pl.BlockSpec((tm, k), lambda n, i: (i, 0)),
                pl.BlockSpec((tn, k), lambda n, i: (n + nb0, 0))]
    args = [a, wt]
    if row_shift:
        in_specs.append(pl.BlockSpec((row_shift, k), lambda n, i: ((n + nb0 + 1) * (tn // row_shift), 0)))
        args.append(wt)
    return pl.pallas_call(
        functools.partial(_mm_wt_kernel, row_shift=row_shift),
        grid=(n_cols // tn, m // tm),
        in_specs=in_specs,
        out_specs=pl.BlockSpec((tm, tn), lambda n, i: (i, n)),
        out_shape=jax.ShapeDtypeStruct((m, n_cols), out_dtype),
        scratch_shapes=[pltpu.VMEM((tn, k), BF16)],
        compiler_params=_params("parallel", "arbitrary"),
        name=name,
    )(*args)


def _mm_rows_kernel(a_ref, w1_ref, w3_ref, o_ref):
    a = a_ref[...]
    gate = jnp.dot(a, w1_ref[...].astype(BF16), preferred_element_type=F32)
    up = jnp.dot(a, w3_ref[...].astype(BF16), preferred_element_type=F32)
    o_ref[...] = (_silu(gate) * up).astype(o_ref.dtype)


def swiglu_up(a, w1, w3, *, tm=2048, tn=256, name="swiglu_up"):
    m, k = a.shape
    n = w1.shape[1]
    assert m % tm == 0 and n % tn == 0
    w_spec = pl.BlockSpec((k, tn), lambda i, j: (0, j))
    return pl.pallas_call(
        _mm_rows_kernel,
        grid=(m // tm, n // tn),
        in_specs=[pl.BlockSpec((tm, k), lambda i, j: (i, 0), pipeline_mode=pl.Buffered(1)),
                  w_spec, w_spec],
        out_specs=pl.BlockSpec((tm, tn), lambda i, j: (i, j)),
        out_shape=jax.ShapeDtypeStruct((m, n), BF16),
        compiler_params=_params("parallel", "arbitrary"),
        name=name,
    )(a, w1, w3)


def _mm_ktiled_kernel(a_ref, w_ref, r_ref, o_ref):
    @pl.when(pl.program_id(2) == 0)
    def _():
        o_ref[...] = r_ref[...]

    o_ref[...] += jnp.dot(a_ref[...], w_ref[...].astype(BF16), preferred_element_type=F32)


def matmul_ktiled_res(a, w, res, *, tm=1024, tn=1024, tk=2048, name="matmul_ktiled"):
    m, k = a.shape
    n = w.shape[1]
    assert m % tm == 0 and n % tn == 0 and k % tk == 0
    return pl.pallas_call(
        _mm_ktiled_kernel,
        grid=(m // tm, n // tn, k // tk),
        in_specs=[pl.BlockSpec((tm, tk), lambda i, j, kk: (i, kk)),
                  pl.BlockSpec((tk, tn), lambda i, j, kk: (kk, j)),
                  pl.BlockSpec((tm, tn), lambda i, j, kk: (i, j))],
        out_specs=pl.BlockSpec((tm, tn), lambda i, j, kk: (i, j)),
        out_shape=jax.ShapeDtypeStruct((m, n), F32),
        compiler_params=_params("parallel", "parallel", "arbitrary"),
        name=name,
    )(a, w, res)


def _moba_kernel(q_ref, k_ref, v_ref, o_ref, kmean_ref, sel_ref, *, blk, nblk, topk, scale, span):
    qb = pl.program_id(2)
    nbp = kmean_ref.shape[0]

    @pl.when(qb == 0)
    def _():
        kmean_ref[...] = jnp.zeros_like(kmean_ref)
        for n in range(nblk):
            kb = k_ref[n * blk:(n + 1) * blk, :].astype(F32)
            kmean_ref[n:n + 1, :] = jnp.mean(kb, axis=0, keepdims=True)

    qs = q_ref[...].astype(F32) * scale
    gate = _nt(kmean_ref[...], qs, precision=HIGHEST)
    bid = lax.broadcasted_iota(jnp.int32, gate.shape, 0)
    t = jnp.where(bid < qb, gate, -jnp.inf)
    sel = jnp.zeros(gate.shape, F32)
    for _ in range(topk):
        mx = jnp.max(t, axis=0, keepdims=True)
        idx = jnp.min(jnp.where(t == mx, bid, nbp), axis=0, keepdims=True)
        pick = (bid == idx) & (mx > -jnp.inf)
        sel = jnp.where(pick, 1.0, sel)
        t = jnp.where(bid == idx, -jnp.inf, t)
    sel_ref[...] = jnp.concatenate([sel, jnp.zeros((LANES - nbp, blk), F32)], axis=0).T

    qsb = qs.astype(BF16)
    row = lax.broadcasted_iota(jnp.int32, (blk, blk), 0)
    col = lax.broadcasted_iota(jnp.int32, (blk, blk), 1)
    ahead = col - row
    for nv in range(span, nblk + 1, span):
        @pl.when((qb >= nv - span) & (qb < nv))
        def _(nv=nv):
            s = _nt(qsb, k_ref[0:nv * blk, :])
            pieces = []
            for kb in range(nv):
                sb = s[:, kb * blk:(kb + 1) * blk]
                if kb >= nv - span:
                    own_limit = jnp.where(qb == kb, 0, -2 * blk)
                    piece = jnp.where(ahead <= own_limit, sb, NEG)
                else:
                    piece = NEG
                pieces.append(jnp.where(sel_ref[:, kb:kb + 1] > 0.5, sb, piece))
            m = jnp.max(functools.reduce(jnp.maximum, pieces), axis=-1, keepdims=True)
            probs = [jnp.exp(piece - m) for piece in pieces]
            l = jnp.sum(functools.reduce(jnp.add, probs), axis=-1, keepdims=True)
            p = jnp.concatenate([pr.astype(BF16) for pr in probs], axis=1)
            o = jnp.dot(p, v_ref[0:nv * blk, :], preferred_element_type=F32)
            o_ref[...] = (o / l).astype(o_ref.dtype)


def moba_attention(qkv, *, nh, dh, span=2):
    bsz, t, _ = qkv.shape
    blk = MOBA_BLOCK
    nblk = t // blk
    nbp = -(-nblk // 8) * 8
    assert t % blk == 0 and nbp <= LANES and dh % LANES == 0 and nblk % span == 0
    kern = functools.partial(_moba_kernel, blk=blk, nblk=nblk, topk=min(MOBA_TOPK, nblk),
                             scale=dh ** -0.5, span=span)
    return pl.pallas_call(
        kern,
        grid=(bsz, nh, nblk),
        in_specs=[pl.BlockSpec((None, blk, dh), lambda b, h, i: (b, i, h)),
                  pl.BlockSpec((None, t, dh), lambda b, h, i: (b, 0, nh + h)),
                  pl.BlockSpec((None, t, dh), lambda b, h, i: (b, 0, 2 * nh + h))],
        out_specs=pl.BlockSpec((None, blk, dh), lambda b, h, i: (b, i, h)),
        out_shape=jax.ShapeDtypeStruct((bsz, t, nh * dh), BF16),
        scratch_shapes=[pltpu.VMEM((nbp, dh), F32),
                        pltpu.VMEM((blk, LANES), F32)],
        compiler_params=_params("parallel", "parallel", "arbitrary"),
        name="moba_attention",
    )(qkv, qkv, qkv)


def _causal_conv(x, buf_ref, w_ref, b_ref, first):
    rows = x.shape[0]

    @pl.when(first)
    def _():
        buf_ref[0:CONV_PAD, :] = jnp.zeros((CONV_PAD, x.shape[1]), F32)

    buf_ref[CONV_PAD:CONV_PAD + rows, :] = x
    y = b_ref[...] + w_ref[CONV_WIDTH - 1:CONV_WIDTH, :] * x
    for kk in range(CONV_WIDTH - 1):
        off = CONV_PAD - (CONV_WIDTH - 1) + kk
        y = y + w_ref[kk:kk + 1, :] * buf_ref[off:off + rows, :]
    buf_ref[0:CONV_PAD, :] = x[rows - CONV_PAD:rows, :]
    return y


def _lru_kernel(xr_ref, gr_ref, cw_ref, cb_ref, wa_ref, ba_ref, wx_ref, bx_ref, lam_ref, o_ref,
                xbuf_ref, h_ref, *, nblocks, bdim):
    ti = pl.program_id(2)
    rows = xr_ref.shape[0]

    @pl.when(ti == 0)
    def _():
        h_ref[...] = jnp.zeros_like(h_ref)

    x = _causal_conv(xr_ref[...], xbuf_ref, cw_ref, cb_ref, ti == 0)
    ga, gx = [], []
    for n in range(nblocks):
        xb = x[:, n * bdim:(n + 1) * bdim].astype(BF16)
        ga.append(jnp.dot(xb, wa_ref[n].astype(BF16), preferred_element_type=F32))
        gx.append(jnp.dot(xb, wx_ref[n].astype(BF16), preferred_element_type=F32))
    gate_a = jnp.concatenate(ga, axis=-1) + ba_ref[...]
    gate_x = jnp.concatenate(gx, axis=-1) + bx_ref[...]
    log_a = -LRU_C * _sigmoid(gate_a) * _softplus(-lam_ref[...])
    a = jnp.exp(log_a)
    u = jnp.sqrt(1.0 - jnp.exp(2.0 * log_a)) * _sigmoid(gate_x) * x

    sub = jnp.bitwise_and(lax.broadcasted_iota(jnp.int32, a.shape, 0), SUBLANES - 1)
    s = 1
    while s < SUBLANES:
        keep = sub >= s
        a_prev = jnp.where(keep, pltpu.roll(a, s, 0), 1.0)
        u_prev = jnp.where(keep, pltpu.roll(u, s, 0), 0.0)
        u = a * u_prev + u
        a = a * a_prev
        s *= 2
    carry = h_ref[0:1, :]
    groups = []
    for gi in range(rows // SUBLANES):
        lo = gi * SUBLANES
        hg = u[lo:lo + SUBLANES] + a[lo:lo + SUBLANES] * carry
        groups.append(hg)
        carry = hg[SUBLANES - 1:SUBLANES, :]
    h = jnp.concatenate(groups, axis=0)
    h_ref[0:1, :] = carry

    g = gr_ref[...]
    gelu = 0.5 * g * (1.0 + jnp.tanh(math.sqrt(2.0 / math.pi) * (g + 0.044715 * g * g * g)))
    o_ref[...] = (h * gelu).astype(o_ref.dtype)


def conv_rglru(xg, conv_w, conv_b, w_a, b_a, w_x, b_x, lam, *, tt=256, tc=512):
    bsz, t, w2 = xg.shape
    w = w2 // 2
    nblocks_all, bdim, _ = w_a.shape
    nb = tc // bdim
    assert t % tt == 0 and w % tc == 0 and tc % bdim == 0
    row = lambda v: v.reshape(1, w)
    vec_spec = pl.BlockSpec((1, tc), lambda b, c, i: (0, c))
    gate_spec = pl.BlockSpec((nb, bdim, bdim), lambda b, c, i: (c, 0, 0))
    kern = functools.partial(_lru_kernel, nblocks=nb, bdim=bdim)
    return pl.pallas_call(
        kern,
        grid=(bsz, w // tc, t // tt),
        in_specs=[pl.BlockSpec((None, tt, tc), lambda b, c, i: (b, i, c)),
                  pl.BlockSpec((None, tt, tc), lambda b, c, i: (b, i, w // tc + c)),
                  pl.BlockSpec((CONV_WIDTH, tc), lambda b, c, i: (0, c)),
                  vec_spec, gate_spec, vec_spec, gate_spec, vec_spec, vec_spec],
        out_specs=pl.BlockSpec((None, tt, tc), lambda b, c, i: (b, i, c)),
        out_shape=jax.ShapeDtypeStruct((bsz, t, w), BF16),
        scratch_shapes=[pltpu.VMEM((CONV_PAD + tt, tc), F32),
                        pltpu.VMEM((8, tc), F32)],
        compiler_params=_params("parallel", "parallel", "arbitrary"),
        name="conv_rglru",
    )(xg, xg, conv_w, row(conv_b), w_a, row(b_a), w_x, row(b_x), row(lam))


def _gla_kernel(q_ref, k_ref, v_ref, r_ref, lr_ref, wa_ref, ba_ref, g_ref, o_ref, st_ref, *,
                chunk, scale, hps, dk, dv):
    ci = pl.program_id(2)

    @pl.when(ci == 0)
    def _():
        st_ref[...] = jnp.zeros_like(st_ref)

    z = jnp.dot(lr_ref[...].astype(BF16), wa_ref[...].astype(BF16),
                preferred_element_type=F32) + ba_ref[...]
    g = (jnp.minimum(z, 0.0) - jnp.log(1.0 + jnp.exp(-jnp.abs(z)))) / GLA_TAU
    row = lax.broadcasted_iota(jnp.int32, (chunk, chunk), 0)
    col = lax.broadcasted_iota(jnp.int32, (chunk, chunk), 1)
    sub0 = row - jnp.bitwise_and(row, GLA_SUB - 1)
    sums = jnp.concatenate([jnp.where(col <= row, 1.0, 0.0), jnp.where(col < sub0, 1.0, 0.0)],
                           axis=0).astype(BF16)
    bb = _dot_exact01(sums, g, ones_on_left=True)
    jrow = lax.broadcasted_iota(jnp.int32, (chunk, dk), 0)
    srow = lax.broadcasted_iota(jnp.int32, (GLA_SUB, chunk), 0)
    scol = lax.broadcasted_iota(jnp.int32, (GLA_SUB, chunk), 1)

    for h in range(hps):
        b = bb[0:chunk, h * dk:(h + 1) * dk]
        b_sub = bb[chunk:2 * chunk, h * dk:(h + 1) * dk]
        q = q_ref[:, h * dk:(h + 1) * dk].astype(F32) * scale
        k = k_ref[:, h * dk:(h + 1) * dk].astype(F32)
        v = v_ref[:, h * dv:(h + 1) * dv]
        b_last = b[chunk - 1:chunk, :]

        o = _nt((q * jnp.exp(b)).astype(BF16), st_ref[h].astype(BF16))

        parts = []
        for i in range(chunk // GLA_SUB):
            lo, hi = i * GLA_SUB, (i + 1) * GLA_SUB
            ref = b_sub[lo:lo + 1, :]
            qi = (q[lo:hi] * jnp.exp(b[lo:hi] - ref)).astype(BF16)
            ki = (k * jnp.exp(jnp.where(jrow < hi, ref - b, 0.0))).astype(BF16)
            att = _nt(qi, ki)
            att = jnp.where(scol <= srow + lo, att, 0.0)
            parts.append(jnp.dot(att.astype(BF16), v, preferred_element_type=F32))
        o = o + jnp.concatenate(parts, axis=0)

        kd = (k * jnp.exp(b_last - b)).astype(BF16)
        upd = lax.dot_general(v, kd, (((0,), (0,)), ((), ())), preferred_element_type=F32)
        st_ref[h] = st_ref[h] * jnp.exp(b_last) + upd

        y = o * lax.rsqrt(jnp.mean(o * o, axis=-1, keepdims=True) + NORM_EPS) * g_ref[...]
        gate = _silu(r_ref[:, h * dv:(h + 1) * dv].astype(F32))
        o_ref[:, h * dv:(h + 1) * dv] = (y * gate).astype(o_ref.dtype)


def gla(p1a, sm, w_a2p, b_a, norm, *, nh, dk, dv, chunk=128, hps=4):
    bsz, t, _ = p1a.shape
    hps = min(hps, nh)
    assert t % chunk == 0 and chunk % GLA_SUB == 0 and nh % hps == 0
    kq = nh * dk
    gk, gv = hps * dk, hps * dv
    koff, voff, roff = kq // gk, 2 * kq // gv, (2 * kq + nh * dv) // gv
    kern = functools.partial(_gla_kernel, chunk=chunk, scale=dk ** -0.5, hps=hps, dk=dk, dv=dv)
    return pl.pallas_call(
        kern,
        grid=(bsz, nh // hps, t // chunk),
        in_specs=[pl.BlockSpec((None, chunk, gk), lambda b, h, c: (b, c, h)),
                  pl.BlockSpec((None, chunk, gk), lambda b, h, c: (b, c, koff + h)),
                  pl.BlockSpec((None, chunk, gv), lambda b, h, c: (b, c, voff + h)),
                  pl.BlockSpec((None, chunk, gv), lambda b, h, c: (b, c, roff + h)),
                  pl.BlockSpec((None, chunk, LANES), lambda b, h, c: (b, c, 0)),
                  pl.BlockSpec((LANES, gk), lambda b, h, c: (0, h)),
                  pl.BlockSpec((1, gk), lambda b, h, c: (0, h)),
                  pl.BlockSpec((1, dv), lambda b, h, c: (0, 0))],
        out_specs=pl.BlockSpec((None, chunk, gv), lambda b, h, c: (b, c, h)),
        out_shape=jax.ShapeDtypeStruct((bsz, t, nh * dv), BF16),
        scratch_shapes=[pltpu.VMEM((hps, dv, dk), F32)],
        compiler_params=_params("parallel", "parallel", "arbitrary"),
        name="gla",
    )(p1a, p1a, p1a, p1a, sm, w_a2p, b_a.reshape(1, kq), norm.reshape(1, dv))


def _ssd_kernel(z_ref, xs_ref, bm_ref, cm_ref, dt_ref, cwx_ref, cwb_ref, cwc_ref, cbx_ref, cbb_ref,
                cbc_ref, dtb_ref, alog_ref, dfull_ref, norm_ref, o_ref,
                xbuf_ref, bbuf_ref, cbuf_ref, st_ref, *, chunk, hpg, hd):
    ci = pl.program_id(2)
    first = ci == 0

    @pl.when(first)
    def _():
        st_ref[...] = jnp.zeros_like(st_ref)

    xs = _silu(_causal_conv(xs_ref[...], xbuf_ref, cwx_ref, cbx_ref, first))
    bm = _silu(_causal_conv(bm_ref[...], bbuf_ref, cwb_ref, cbb_ref, first))
    cm = _silu(_causal_conv(cm_ref[...], cbuf_ref, cwc_ref, cbc_ref, first))
    width = hpg * hd

    dt = _softplus(dt_ref[...] + dtb_ref[...])
    da = dt * (-jnp.exp(alog_ref[...]))
    row = lax.broadcasted_iota(jnp.int32, (chunk, chunk), 0)
    col = lax.broadcasted_iota(jnp.int32, (chunk, chunk), 1)
    causal = col <= row
    tril = jnp.where(causal, 1.0, 0.0).astype(BF16)
    cs = _dot_exact01(tril, da, ones_on_left=True)
    cs_t = cs.T

    erow = lax.broadcasted_iota(jnp.int32, (LANES, width), 0)
    ecol = lax.broadcasted_iota(jnp.int32, (LANES, width), 1)
    expand = jnp.where((ecol >= erow * hd) & (ecol < (erow + 1) * hd), 1.0, 0.0).astype(BF16)
    full = _dot_exact01(expand, jnp.concatenate([cs, dt], axis=0), ones_on_left=False)
    cs_full = full[0:chunk]
    dt_full = full[chunk:2 * chunk]
    cs_last = cs_full[chunk - 1:chunk, :]

    xd = xs * dt_full
    xd_b = xd.astype(BF16)
    bm_b = bm.astype(BF16)
    cm_b = cm.astype(BF16)
    cb = _nt(cm_b, bm_b)

    lane = lax.broadcasted_iota(jnp.int32, (chunk, 2 * hd), 1)
    slabs = []
    for j in range(hpg // 2):
        xpair = xd_b[:, 2 * j * hd:(2 * j + 2) * hd]
        acc = None
        for half in range(2):
            h = 2 * j + half
            seg = cs[:, h:h + 1] - cs_t[h:h + 1, :]
            lmat = jnp.exp(jnp.where(causal, seg, NEG))
            mh = (cb * lmat).astype(BF16)
            mine = (lane >= half * hd) & (lane < (half + 1) * hd)
            part = jnp.dot(mh, jnp.where(mine, xpair, jnp.zeros_like(xpair)),
                           preferred_element_type=F32)
            acc = part if acc is None else acc + part
        slabs.append(acc)
    y = jnp.concatenate(slabs, axis=-1)

    st = st_ref[...]
    y = y + jnp.dot(cm_b, st.astype(BF16), preferred_element_type=F32) * jnp.exp(cs_full)
    xdd = (xd * jnp.exp(cs_last - cs_full)).astype(BF16)
    upd = lax.dot_general(bm_b, xdd, (((0,), (0,)), ((), ())), preferred_element_type=F32)
    st_ref[...] = st * jnp.exp(cs_last) + upd

    y = y + dfull_ref[...] * xs
    y = y * _silu(z_ref[...])
    y = y * lax.rsqrt(jnp.mean(y * y, axis=-1, keepdims=True) + NORM_EPS) * norm_ref[...]
    o_ref[...] = y.astype(o_ref.dtype)


def ssd(p1b, sm, conv_w, conv_b, dt_bias, a_log, d_skip, norm, *, nheads, hd, groups, nstate,
        chunk=256):
    bsz, t, _ = p1b.shape
    width = nheads * hd
    gw = width // groups
    hpg = nheads // groups
    assert t % chunk == 0 and nstate == LANES and gw % LANES == 0 and hpg % 2 == 0
    xoff = width // gw
    boff = 2 * width // nstate
    coff = boff + groups
    cxo, cbo, cco = 0, width // nstate, width // nstate + groups
    padl = lambda v: jnp.pad(v.reshape(groups, 1, hpg), ((0, 0), (0, 0), (0, LANES - hpg)))
    dfull = jnp.repeat(d_skip.reshape(groups, hpg), hd, axis=1).reshape(groups, 1, gw)
    cb2 = conv_b.reshape(1, -1)
    g3 = lambda b, g, c: (g, 0, 0)
    kern = functools.partial(_ssd_kernel, chunk=chunk, hpg=hpg, hd=hd)
    return pl.pallas_call(
        kern,
        grid=(bsz, groups, t // chunk),
        in_specs=[pl.BlockSpec((None, chunk, gw), lambda b, g, c: (b, c, g)),
                  pl.BlockSpec((None, chunk, gw), lambda b, g, c: (b, c, xoff + g)),
                  pl.BlockSpec((None, chunk, nstate), lambda b, g, c: (b, c, boff + g)),
                  pl.BlockSpec((None, chunk, nstate), lambda b, g, c: (b, c, coff + g)),
                  pl.BlockSpec((None, chunk, LANES), lambda b, g, c: (b, c, 1 + g)),
                  pl.BlockSpec((CONV_WIDTH, gw), lambda b, g, c: (0, cxo + g)),
                  pl.BlockSpec((CONV_WIDTH, nstate), lambda b, g, c: (0, cbo + g)),
                  pl.BlockSpec((CONV_WIDTH, nstate), lambda b, g, c: (0, cco + g)),
                  pl.BlockSpec((1, gw), lambda b, g, c: (0, cxo + g)),
                  pl.BlockSpec((1, nstate), lambda b, g, c: (0, cbo + g)),
                  pl.BlockSpec((1, nstate), lambda b, g, c: (0, cco + g)),
                  pl.BlockSpec((None, 1, LANES), g3),
                  pl.BlockSpec((None, 1, LANES), g3),
                  pl.BlockSpec((None, 1, gw), g3),
                  pl.BlockSpec((1, gw), lambda b, g, c: (0, g))],
        out_specs=pl.BlockSpec((None, chunk, gw), lambda b, g, c: (b, c, g)),
        out_shape=jax.ShapeDtypeStruct((bsz, t, width), BF16),
        scratch_shapes=[pltpu.VMEM((CONV_PAD + chunk, gw), F32),
                        pltpu.VMEM((CONV_PAD + chunk, nstate), F32),
                        pltpu.VMEM((CONV_PAD + chunk, nstate), F32),
                        pltpu.VMEM((nstate, gw), F32)],
        compiler_params=_params("parallel", "parallel", "arbitrary"),
        name="ssd",
    )(p1b, p1b, p1b, p1b, sm, conv_w, conv_w, conv_w, cb2, cb2, cb2,
      padl(dt_bias), padl(a_log), dfull, norm.reshape(1, width))


def _norm_router_kernel(x_ref, g_ref, r_ref, h_ref, route_ref, *, n_exp):
    x = x_ref[...]
    y = x * lax.rsqrt(jnp.mean(x * x, axis=-1, keepdims=True) + NORM_EPS) * g_ref[...]
    h_ref[...] = y
    logits = jnp.dot(y, r_ref[...], precision=HIGHEST, preferred_element_type=F32)
    lane = lax.broadcasted_iota(jnp.int32, logits.shape, 1)
    t = jnp.where(lane < n_exp, logits, -jnp.inf)
    m1 = jnp.max(t, axis=-1, keepdims=True)
    i1 = jnp.min(jnp.where(t == m1, lane, LANES), axis=-1, keepdims=True)
    t2 = jnp.where(lane == i1, -jnp.inf, t)
    m2 = jnp.max(t2, axis=-1, keepdims=True)
    i2 = jnp.min(jnp.where(t2 == m2, lane, LANES), axis=-1, keepdims=True)
    e = jnp.exp(m2 - m1)
    w1 = 1.0 / (1.0 + e)
    w2 = e / (1.0 + e)
    route = jnp.where(lane == 0, i1.astype(F32),
                      jnp.where(lane == 1, i2.astype(F32),
                                jnp.where(lane == 2, w1, jnp.where(lane == 3, w2, 0.0))))
    route_ref[...] = route


def norm_router(x, g, router, tm=256):
    m, d = x.shape
    n_exp = router.shape[1]
    rp = jnp.pad(router, ((0, 0), (0, LANES - n_exp)))
    return pl.pallas_call(
        functools.partial(_norm_router_kernel, n_exp=n_exp),
        grid=(m // tm,),
        in_specs=[pl.BlockSpec((tm, d), lambda i: (i, 0)),
                  pl.BlockSpec((1, d), lambda i: (0, 0)),
                  pl.BlockSpec((d, LANES), lambda i: (0, 0))],
        out_specs=[pl.BlockSpec((tm, d), lambda i: (i, 0)),
                   pl.BlockSpec((tm, LANES), lambda i: (i, 0))],
        out_shape=[jax.ShapeDtypeStruct((m, d), F32), jax.ShapeDtypeStruct((m, LANES), F32)],
        compiler_params=_params("parallel"),
        name="norm_router",
    )(x, g.reshape(1, d), rp)


def _row_gather(idx_ref, base, src_hbm, dst_ref, sem, rows):
    def start(i, c):
        for j in range(GATHER_UNROLL):
            r = i * GATHER_UNROLL + j
            pltpu.make_async_copy(src_hbm.at[pl.ds(idx_ref[base + r], 1)],
                                  dst_ref.at[pl.ds(r, 1)], sem).start(priority=j % 2)
        return c

    lax.fori_loop(0, rows // GATHER_UNROLL, start, 0)
    pltpu.make_async_copy(src_hbm.at[pl.ds(0, rows)], dst_ref, sem).wait()


def _dispatch_kernel(tok_ref, live_ref, h_hbm, o_ref, buf_ref, sem, *, sub):
    i = pl.program_id(0)
    rows = buf_ref.shape[0]
    live = live_ref[i // sub] - (i % sub) * rows

    @pl.when(live > 0)
    def _():
        _row_gather(tok_ref, i * rows, h_hbm, buf_ref, sem, rows)
        o_ref[...] = buf_ref[...].astype(o_ref.dtype)

    @pl.when(live <= 0)
    def _():
        o_ref[...] = jnp.zeros_like(o_ref)


def moe_dispatch(row_token, tile_rows, h, *, tile, tm=256):
    r = row_token.shape[0]
    d = h.shape[1]
    assert tm % GATHER_UNROLL == 0 and tile % tm == 0 and r % tile == 0
    return pl.pallas_call(
        functools.partial(_dispatch_kernel, sub=tile // tm),
        grid_spec=pltpu.PrefetchScalarGridSpec(
            num_scalar_prefetch=2,
            grid=(r // tm,),
            in_specs=[pl.BlockSpec(memory_space=pl.ANY)],
            out_specs=pl.BlockSpec((tm, d), lambda i, tok, live: (i, 0)),
            scratch_shapes=[pltpu.VMEM((tm, d), F32), pltpu.SemaphoreType.DMA(())]),
        out_shape=jax.ShapeDtypeStruct((r, d), BF16),
        compiler_params=_params("arbitrary"),
        name="moe_dispatch",
    )(row_token, tile_rows, h)


def _moe_mm_kernel(te_ref, tv_ref, x_ref, *refs, act_pair):
    n_w = 2 if act_pair else 1
    w_refs = refs[:n_w]
    o_ref = refs[n_w]
    r = pl.program_id(1)

    live = tv_ref[r]
    tm = x_ref.shape[0]
    half = tm // 2

    def compute(nrows):
        x = x_ref[0:nrows, :]
        out = jnp.dot(x, w_refs[0][...].astype(BF16), preferred_element_type=F32)
        if act_pair:
            out = _silu(out) * jnp.dot(x, w_refs[1][...].astype(BF16), preferred_element_type=F32)
        o_ref[0:nrows, :] = out.astype(o_ref.dtype)

    @pl.when(live > half)
    def _():
        compute(tm)

    @pl.when((live > 0) & (live <= half))
    def _():
        compute(half)
        o_ref[half:tm, :] = jnp.zeros((tm - half, o_ref.shape[1]), o_ref.dtype)

    @pl.when(live == 0)
    def _():
        o_ref[...] = jnp.zeros_like(o_ref)


def moe_matmul(tile_expert, tile_rows, x, w_list, *, out_dtype, tm=MOE_ROW_TILE, tn=256,
               name="moe_matmul"):
    r, k = x.shape
    n = w_list[0].shape[2]
    assert r % tm == 0 and n % tn == 0
    kern = functools.partial(_moe_mm_kernel, act_pair=len(w_list) == 2)
    w_spec = pl.BlockSpec((None, k, tn), lambda j, i, te, tv: (te[i], 0, j))
    return pl.pallas_call(
        kern,
        grid_spec=pltpu.PrefetchScalarGridSpec(
            num_scalar_prefetch=2,
            grid=(n // tn, r // tm),
            in_specs=[pl.BlockSpec((tm, k), lambda j, i, te, tv: (i, 0))] + [w_spec] * len(w_list),
            out_specs=pl.BlockSpec((tm, tn), lambda j, i, te, tv: (i, j))),
        out_shape=jax.ShapeDtypeStruct((r, n), out_dtype),
        compiler_params=_params("parallel", "arbitrary"),
        name=name,
    )(tile_expert, tile_rows, x, *w_list)


def _combine_kernel(d0_ref, d1_ref, x_ref, route_ref, g_ref, y_hbm, o_ref, buf0_ref, buf1_ref, sems):
    rows = x_ref.shape[0]
    base = pl.program_id(0) * rows
    _row_gather(d0_ref, base, y_hbm, buf0_ref, sems.at[0], rows)
    _row_gather(d1_ref, base, y_hbm, buf1_ref, sems.at[1], rows)
    w0 = route_ref[:, 2:3]
    w1 = route_ref[:, 3:4]
    x = x_ref[...] + w0 * buf0_ref[...] + w1 * buf1_ref[...]
    y = x * lax.rsqrt(jnp.mean(x * x, axis=-1, keepdims=True) + NORM_EPS)
    o_ref[...] = y * g_ref[...]


def moe_combine_norm(dest0, dest1, x, route, y, g, tm=256):
    m, d = x.shape
    assert tm % GATHER_UNROLL == 0
    return pl.pallas_call(
        _combine_kernel,
        grid_spec=pltpu.PrefetchScalarGridSpec(
            num_scalar_prefetch=2,
            grid=(m // tm,),
            in_specs=[pl.BlockSpec((tm, d), lambda i, a, b: (i, 0)),
                      pl.BlockSpec((tm, LANES), lambda i, a, b: (i, 0)),
                      pl.BlockSpec((1, d), lambda i, a, b: (0, 0)),
                      pl.BlockSpec(memory_space=pl.ANY)],
            out_specs=pl.BlockSpec((tm, d), lambda i, a, b: (i, 0)),
            scratch_shapes=[pltpu.VMEM((tm, d), F32), pltpu.VMEM((tm, d), F32),
                            pltpu.SemaphoreType.DMA((2,))]),
        out_shape=jax.ShapeDtypeStruct((m, d), F32),
        compiler_params=_params("arbitrary"),
        name="moe_combine_norm",
    )(dest0, dest1, x, route, g.reshape(1, d), y)


def _moe_plan(ids, tm):
    n_tok = ids.shape[0]
    e_flat = ids.reshape(-1)
    onehot = (e_flat[:, None] == jnp.arange(N_EXPERTS, dtype=jnp.int32)[None, :]).astype(jnp.int32)
    csum = jnp.cumsum(onehot, axis=0)
    pos = jnp.sum(csum * onehot, axis=1) - 1
    counts = csum[-1]
    padded = ((counts + tm - 1) // tm) * tm
    gend = jnp.cumsum(padded)
    gstart = gend - padded
    dest = gstart[e_flat] + pos
    n_rows = 2 * n_tok + N_EXPERTS * tm
    row_token = jnp.zeros((n_rows,), jnp.int32).at[dest].set(
        jnp.arange(2 * n_tok, dtype=jnp.int32) // 2)
    tile_start = jnp.arange(n_rows // tm, dtype=jnp.int32) * tm
    last_start = jnp.maximum(gend[-1] - tm, 0)
    probe = jnp.minimum(tile_start, last_start)
    tile_expert = jnp.minimum(jnp.sum((probe[:, None] >= gend[None, :]).astype(jnp.int32), axis=1),
                              N_EXPERTS - 1).astype(jnp.int32)
    live_end = (gstart + counts)[tile_expert]
    tile_rows = jnp.where(tile_start < gend[-1], jnp.clip(live_end - tile_start, 0, tm), 0)
    dest2 = dest.reshape(n_tok, 2).astype(jnp.int32)
    return row_token, tile_expert, tile_rows.astype(jnp.int32), dest2[:, 0], dest2[:, 1]


def kernel(x, ev_norm1, ev_w_in, ev_lru_conv_w, ev_lru_conv_b, ev_lru_w_a, ev_lru_b_a, ev_lru_w_x, ev_lru_b_x, ev_lru_lambda, ev_w_out, ev_norm2, ev_ffn_w1, ev_ffn_w3, ev_ffn_w2, od_norm1, od_w_in, od_gla_w_a2, od_gla_b_a, od_gla_norm, od_ssd_conv_w, od_ssd_conv_b, od_ssd_dt_bias, od_ssd_a_log, od_ssd_d, od_ssd_norm, od_w_out, od_norm2, od_router, od_moe_w1, od_moe_w3, od_moe_w2, final_norm):
    bsz, t, d = x.shape
    n_tok = bsz * t
    xf = x.reshape(n_tok, d)

    lru_w = ev_lru_lambda.shape[1]
    moba_w = (ev_w_in.shape[2] - 2 * lru_w) // 3
    moba_dh = LANES
    h = rmsnorm(xf, ev_norm1[0])
    qkv = matmul([h], [ev_w_in[0]], w_col0=0, n_cols=3 * moba_w, out_dtype=BF16, name="l0_in_qkv")
    xg = matmul([h], [ev_w_in[0]], w_col0=3 * moba_w, n_cols=2 * lru_w, out_dtype=F32, name="l0_in_lru")
    att = moba_attention(qkv.reshape(bsz, t, 3 * moba_w), nh=moba_w // moba_dh, dh=moba_dh)
    rec = conv_rglru(xg.reshape(bsz, t, 2 * lru_w), ev_lru_conv_w[0], ev_lru_conv_b[0], ev_lru_w_a[0],
                     ev_lru_b_a[0], ev_lru_w_x[0], ev_lru_b_x[0], ev_lru_lambda[0])
    xf = matmul([att.reshape(n_tok, moba_w), rec.reshape(n_tok, lru_w)], [ev_w_out[0]], res=xf,
                name="l0_out")
    h = rmsnorm(xf, ev_norm2[0])
    gact = swiglu_up(h, ev_ffn_w1[0], ev_ffn_w3[0], name="l0_ffn_up")
    xf = matmul_ktiled_res(gact, ev_ffn_w2[0], xf, name="l0_ffn_down")

    gla_key = od_gla_w_a2.shape[2]
    gla_rank = od_gla_w_a2.shape[1]
    gla_dv = od_gla_norm.shape[1]
    gla_heads = gla_key // (gla_dv // 2)
    gla_val = gla_heads * gla_dv
    ssd_heads = od_ssd_a_log.shape[1]
    ssd_width = od_ssd_norm.shape[1]
    ssd_conv_dim = od_ssd_conv_w.shape[2]
    ssd_groups = 4
    ssd_state = (ssd_conv_dim - ssd_width) // (2 * ssd_groups)
    hpg = ssd_heads // ssd_groups
    w_in1_t = od_w_in[0].T
    c_lr = 2 * gla_key + 2 * gla_val
    c_z = c_lr + gla_rank
    c_dt = c_z + ssd_width + ssd_conv_dim
    zpad = lambda n: jnp.zeros((n, d), F32)
    w_small_t = jnp.concatenate(
        [w_in1_t[c_lr:c_z], zpad(LANES - gla_rank)]
        + [blk for g in range(ssd_groups)
           for blk in (w_in1_t[c_dt + g * hpg:c_dt + (g + 1) * hpg], zpad(LANES - hpg))], axis=0)

    h = rmsnorm(xf, od_norm1[0])
    p1a = matmul_wt(h, w_in1_t, row0=0, n_cols=c_lr, out_dtype=BF16, name="l1_in_gla")
    p1b = matmul_wt(h, w_in1_t, row0=c_z, n_cols=c_dt - c_z, out_dtype=F32, name="l1_in_ssd")
    sm = matmul_wt(h, w_small_t, out_dtype=F32, tn=w_small_t.shape[0], name="l1_in_small")
    w_a2p = jnp.pad(od_gla_w_a2[0], ((0, LANES - gla_rank), (0, 0)))
    o_gla = gla(p1a.reshape(bsz, t, c_lr), sm.reshape(bsz, t, -1), w_a2p, od_gla_b_a[0], od_gla_norm[0],
                nh=gla_heads, dk=gla_key // gla_heads, dv=gla_dv)
    y_ssd = ssd(p1b.reshape(bsz, t, -1), sm.reshape(bsz, t, -1), od_ssd_conv_w[0], od_ssd_conv_b[0],
                od_ssd_dt_bias[0], od_ssd_a_log[0], od_ssd_d[0], od_ssd_norm[0],
                nheads=ssd_heads, hd=ssd_width // ssd_heads, groups=ssd_groups, nstate=ssd_state)
    xf = matmul([o_gla.reshape(n_tok, gla_val), y_ssd.reshape(n_tok, ssd_width)], [od_w_out[0]], res=xf,
                name="l1_out")

    hf, route = norm_router(xf, od_norm2[0], od_router[0])
    ids = route[:, 0:2].astype(jnp.int32)
    row_token, tile_expert, tile_rows, dest0, dest1 = _moe_plan(ids, MOE_ROW_TILE)
    xs = moe_dispatch(row_token, tile_rows, hf, tile=MOE_ROW_TILE)
    gact = moe_matmul(tile_expert, tile_rows, xs, [od_moe_w1[0], od_moe_w3[0]], out_dtype=BF16,
                      tn=MOE_COL_TILE, name="moe_up")
    ye = moe_matmul(tile_expert, tile_rows, gact, [od_moe_w2[0]], out_dtype=F32,
                    tn=MOE_COL_TILE, name="moe_down")
    out = moe_combine_norm(dest0, dest1, xf, route, ye, final_norm)
    return out.reshape(bsz, t, d)
```

```python
import functools
import math

import jax
import jax.numpy as jnp
from jax import lax
from jax.experimental import pallas as pl
from jax.experimental.pallas import tpu as pltpu

F32 = jnp.float32
BF16 = jnp.bfloat16
HIGHEST = lax.Precision.HIGHEST

NORM_EPS = 1e-6
NEG = -1e30
LANES = 128
SUBLANES = 8
VMEM_LIMIT = 56 * 1024 * 1024

CONV_WIDTH = 4
CONV_PAD = 8
MOBA_BLOCK = 256
MOBA_TOPK = 3
LRU_C = 8.0
GLA_TAU = 16.0
GLA_SUB = 16
N_EXPERTS = 8
MOE_ROW_TILE = 512
MOE_COL_TILE = 512
GATHER_UNROLL = 8


def _nt(a, b, precision=None):
    return lax.dot_general(a, b, (((1,), (1,)), ((), ())), precision=precision,
                           preferred_element_type=F32)


def _dot_exact01(ones, x, *, ones_on_left):
    hi = x.astype(BF16)
    r1 = x - hi.astype(F32)
    mid = r1.astype(BF16)
    lo = (r1 - mid.astype(F32)).astype(BF16)
    out = None
    for term in (hi, mid, lo):
        part = (jnp.dot(ones, term, preferred_element_type=F32) if ones_on_left
                else jnp.dot(term, ones, preferred_element_type=F32))
        out = part if out is None else out + part
    return out


def _sigmoid(x):
    return 1.0 / (1.0 + jnp.exp(-x))


def _silu(x):
    return x * _sigmoid(x)


def _softplus(x):
    return jnp.maximum(x, 0.0) + jnp.log(1.0 + jnp.exp(-jnp.abs(x)))


def _params(*sem):
    return pltpu.CompilerParams(dimension_semantics=sem, vmem_limit_bytes=VMEM_LIMIT)


def _rmsnorm_kernel(x_ref, g_ref, o_ref):
    x = x_ref[...]
    y = x * lax.rsqrt(jnp.mean(x * x, axis=-1, keepdims=True) + NORM_EPS)
    o_ref[...] = (y * g_ref[...]).astype(o_ref.dtype)


def rmsnorm(x, g, out_dtype=BF16, tm=256):
    m, d = x.shape
    return pl.pallas_call(
        _rmsnorm_kernel,
        grid=(m // tm,),
        in_specs=[pl.BlockSpec((tm, d), lambda i: (i, 0)),
                  pl.BlockSpec((1, d), lambda i: (0, 0))],
        out_specs=pl.BlockSpec((tm, d), lambda i: (i, 0)),
        out_shape=jax.ShapeDtypeStruct((m, d), out_dtype),
        compiler_params=_params("parallel"),
        name="rmsnorm",
    )(x, g.reshape(1, d))


def _mm_kernel(*refs, n_a, has_res, act_pair):
    a_refs = refs[:n_a]
    n_w = 2 if act_pair else 1
    w_refs = refs[n_a:n_a + n_w]
    pos = n_a + n_w
    r_ref = refs[pos] if has_res else None
    pos += int(has_res)
    o_ref = refs[pos]
    wb_refs = refs[pos + 1:pos + 1 + n_w]

    @pl.when(pl.program_id(1) == 0)
    def _():
        for w_ref, wb_ref in zip(w_refs, wb_refs):
            wb_ref[...] = w_ref[...].astype(BF16)

    def contract(wb_ref):
        acc = None
        k0 = 0
        for a_ref in a_refs:
            ka = a_ref.shape[1]
            part = jnp.dot(a_ref[...], wb_ref[k0:k0 + ka, :], preferred_element_type=F32)
            acc = part if acc is None else acc + part
            k0 += ka
        return acc

    if act_pair:
        out = _silu(contract(wb_refs[0])) * contract(wb_refs[1])
    else:
        out = contract(wb_refs[0])
    if has_res:
        out = out + r_ref[...]
    o_ref[...] = out.astype(o_ref.dtype)


def matmul(a_list, w_list, *, w_col0=0, n_cols=None, res=None, out_dtype=F32, tm=1024, tn=512,
           name="matmul"):
    m = a_list[0].shape[0]
    k = sum(a.shape[1] for a in a_list)
    assert all(w.shape[0] == k for w in w_list)
    n_cols = w_list[0].shape[1] - w_col0 if n_cols is None else n_cols
    assert m % tm == 0 and n_cols % tn == 0 and w_col0 % tn == 0
    nb0 = w_col0 // tn
    in_specs = [pl.BlockSpec((tm, a.shape[1]), lambda n, i: (i, 0)) for a in a_list]
    in_specs += [pl.BlockSpec((k, tn), lambda n, i: (0, n + nb0)) for _ in w_list]
    args = list(a_list) + list(w_list)
    if res is not None:
        in_specs.append(pl.BlockSpec((tm, tn), lambda n, i: (i, n)))
        args.append(res)
    kern = functools.partial(_mm_kernel, n_a=len(a_list), has_res=res is not None,
                             act_pair=len(w_list) == 2)
    return pl.pallas_call(
        kern,
        grid=(n_cols // tn, m // tm),
        in_specs=in_specs,
        out_specs=pl.BlockSpec((tm, tn), lambda n, i: (i, n)),
        out_shape=jax.ShapeDtypeStruct((m, n_cols), out_dtype),
        scratch_shapes=[pltpu.VMEM((k, tn), BF16) for _ in w_list],
        compiler_params=_params("parallel", "arbitrary"),
        name=name,
    )(*args)


def _mm_wt_kernel(a_ref, w_ref, *rest, row_shift):
    if row_shift:
        wn_ref, o_ref, wb_ref = rest
    else:
        o_ref, wb_ref = rest

    @pl.when(pl.program_id(1) == 0)
    def _():
        tn = w_ref.shape[0]
        if row_shift:
            wb_ref[0:tn - row_shift, :] = w_ref[row_shift:tn, :].astype(BF16)
            wb_ref[tn - row_shift:tn, :] = wn_ref[...].astype(BF16)
        else:
            wb_ref[...] = w_ref[...].astype(BF16)

    o_ref[...] = _nt(a_ref[...], wb_ref[...]).astype(o_ref.dtype)


def matmul_wt(a, wt, *, row0=0, n_cols=None, out_dtype=F32, tm=1024, tn=512, name="matmul_wt"):
    m, k = a.shape
    assert wt.shape[1] == k
    n_cols = wt.shape[0] - row0 if n_cols is None else n_cols
    row_shift = row0 % tn
    nb0 = (row0 - row_shift) // tn
    assert m % tm == 0 and n_cols % tn == 0 and row_shift % 16 == 0 and (not row_shift or tn % row_shift == 0)
    in_specs = [pl.BlockSpec((tm, k), lambda n, i: (i, 0)),
                pl.BlockSpec((tn, k), lambda n, i: (n + nb0, 0))]
    args = [a, wt]
    if row_shift:
        in_specs.append(pl.BlockSpec((row_shift, k), lambda n, i: ((n + nb0 + 1) * (tn // row_shift), 0)))
        args.append(wt)
    return pl.pallas_call(
        functools.partial(_mm_wt_kernel, row_shift=row_shift),
        grid=(n_cols // tn, m // tm),
        in_specs=in_specs,
        out_specs=pl.BlockSpec((tm, tn), lambda n, i: (i, n)),
        out_shape=jax.ShapeDtypeStruct((m, n_cols), out_dtype),
        scratch_shapes=[pltpu.VMEM((tn, k), BF16)],
        compiler_params=_params("parallel", "arbitrary"),
        name=name,
    )(*args)


def _mm_rows_kernel(a_ref, w1_ref, w3_ref, o_ref):
    a = a_ref[...]
    gate = jnp.dot(a, w1_ref[...].astype(BF16), preferred_element_type=F32)
    up = jnp.dot(a, w3_ref[...].astype(BF16), preferred_element_type=F32)
    o_ref[...] = (_silu(gate) * up).astype(o_ref.dtype)


def swiglu_up(a, w1, w3, *, tm=2048, tn=256, name="swiglu_up"):
    m, k = a.shape
    n = w1.shape[1]
    assert m % tm == 0 and n % tn == 0
    w_spec = pl.BlockSpec((k, tn), lambda i, j: (0, j))
    return pl.pallas_call(
        _mm_rows_kernel,
        grid=(m // tm, n // tn),
        in_specs=[pl.BlockSpec((tm, k), lambda i, j: (i, 0), pipeline_mode=pl.Buffered(1)),
                  w_spec, w_spec],
        out_specs=pl.BlockSpec((tm, tn), lambda i, j: (i, j)),
        out_shape=jax.ShapeDtypeStruct((m, n), BF16),
        compiler_params=_params("parallel", "arbitrary"),
        name=name,
    )(a, w1, w3)


def _mm_ktiled_kernel(a_ref, w_ref, r_ref, o_ref):
    @pl.when(pl.program_id(2) == 0)
    def _():
        o_ref[...] = r_ref[...]

    o_ref[...] += jnp.dot(a_ref[...], w_ref[...].astype(BF16), preferred_element_type=F32)


def matmul_ktiled_res(a, w, res, *, tm=1024, tn=1024, tk=2048, name="matmul_ktiled"):
    m, k = a.shape
    n = w.shape[1]
    assert m % tm == 0 and n % tn == 0 and k % tk == 0
    return pl.pallas_call(
        _mm_ktiled_kernel,
        grid=(m // tm, n // tn, k // tk),
        in_specs=[pl.BlockSpec((tm, tk), lambda i, j, kk: (i, kk)),
                  pl.BlockSpec((tk, tn), lambda i, j, kk: (kk, j)),
                  pl.BlockSpec((tm, tn), lambda i, j, kk: (i, j))],
        out_specs=pl.BlockSpec((tm, tn), lambda i, j, kk: (i, j)),
        out_shape=jax.ShapeDtypeStruct((m, n), F32),
        compiler_params=_params("parallel", "parallel", "arbitrary"),
        name=name,
    )(a, w, res)


def _moba_kernel(q_ref, k_ref, v_ref, o_ref, kmean_ref, sel_ref, *, blk, nblk, topk, scale, span):
    qb = pl.program_id(2)
    nbp = kmean_ref.shape[0]

    @pl.when(qb == 0)
    def _():
        kmean_ref[...] = jnp.zeros_like(kmean_ref)
        for n in range(nblk):
            kb = k_ref[n * blk:(n + 1) * blk, :].astype(F32)
            kmean_ref[n:n + 1, :] = jnp.mean(kb, axis=0, keepdims=True)

    qs = q_ref[...].astype(F32) * scale
    gate = _nt(kmean_ref[...], qs, precision=HIGHEST)
    bid = lax.broadcasted_iota(jnp.int32, gate.shape, 0)
    t = jnp.where(bid < qb, gate, -jnp.inf)
    sel = jnp.zeros(gate.shape, F32)
    for _ in range(topk):
        mx = jnp.max(t, axis=0, keepdims=True)
        idx = jnp.min(jnp.where(t == mx, bid, nbp), axis=0, keepdims=True)
        pick = (bid == idx) & (mx > -jnp.inf)
        sel = jnp.where(pick, 1.0, sel)
        t = jnp.where(bid == idx, -jnp.inf, t)
    sel_ref[...] = jnp.concatenate([sel, jnp.zeros((LANES - nbp, blk), F32)], axis=0).T

    qsb = qs.astype(BF16)
    row = lax.broadcasted_iota(jnp.int32, (blk, blk), 0)
    col = lax.broadcasted_iota(jnp.int32, (blk, blk), 1)
    ahead = col - row
    for nv in range(span, nblk + 1, span):
        @pl.when((qb >= nv - span) & (qb < nv))
        def _(nv=nv):
            s = _nt(qsb, k_ref[0:nv * blk, :])
            pieces = []
            for kb in range(nv):
                sb = s[:, kb * blk:(kb + 1) * blk]
                if kb >= nv - span:
                    own_limit = jnp.where(qb == kb, 0, -2 * blk)
                    piece = jnp.where(ahead <= own_limit, sb, NEG)
                else:
                    piece = NEG
                pieces.append(jnp.where(sel_ref[:, kb:kb + 1] > 0.5, sb, piece))
            m = jnp.max(functools.reduce(jnp.maximum, pieces), axis=-1, keepdims=True)
            probs = [jnp.exp(piece - m) for piece in pieces]
            l = jnp.sum(functools.reduce(jnp.add, probs), axis=-1, keepdims=True)
            p = jnp.concatenate([pr.astype(BF16) for pr in probs], axis=1)
            o = jnp.dot(p, v_ref[0:nv * blk, :], preferred_element_type=F32)
            o_ref[...] = (o / l).astype(o_ref.dtype)


def moba_attention(qkv, *, nh, dh, span=2):
    bsz, t, _ = qkv.shape
    blk = MOBA_BLOCK
    nblk = t // blk
    nbp = -(-nblk // 8) * 8
    assert t % blk == 0 and nbp <= LANES and dh % LANES == 0 and nblk % span == 0
    kern = functools.partial(_moba_kernel, blk=blk, nblk=nblk, topk=min(MOBA_TOPK, nblk),
                             scale=dh ** -0.5, span=span)
    return pl.pallas_call(
        kern,
        grid=(bsz, nh, nblk),
        in_specs=[pl.BlockSpec((None, blk, dh), lambda b, h, i: (b, i, h)),
                  pl.BlockSpec((None, t, dh), lambda b, h, i: (b, 0, nh + h)),
                  pl.BlockSpec((None, t, dh), lambda b, h, i: (b, 0, 2 * nh + h))],
        out_specs=pl.BlockSpec((None, blk, dh), lambda b, h, i: (b, i, h)),
        out_shape=jax.ShapeDtypeStruct((bsz, t, nh * dh), BF16),
        scratch_shapes=[pltpu.VMEM((nbp, dh), F32),
                        pltpu.VMEM((blk, LANES), F32)],
        compiler_params=_params("parallel", "parallel", "arbitrary"),
        name="moba_attention",
    )(qkv, qkv, qkv)


def _causal_conv(x, buf_ref, w_ref, b_ref, first):
    rows = x.shape[0]

    @pl.when(first)
    def _():
        buf_ref[0:CONV_PAD, :] = jnp.zeros((CONV_PAD, x.shape[1]), F32)

    buf_ref[CONV_PAD:CONV_PAD + rows, :] = x
    y = b_ref[...] + w_ref[CONV_WIDTH - 1:CONV_WIDTH, :] * x
    for kk in range(CONV_WIDTH - 1):
        off = CONV_PAD - (CONV_WIDTH - 1) + kk
        y = y + w_ref[kk:kk + 1, :] * buf_ref[off:off + rows, :]
    buf_ref[0:CONV_PAD, :] = x[rows - CONV_PAD:rows, :]
    return y


def _lru_kernel(xr_ref, gr_ref, cw_ref, cb_ref, wa_ref, ba_ref, wx_ref, bx_ref, lam_ref, o_ref,
                xbuf_ref, h_ref, *, nblocks, bdim):
    ti = pl.program_id(2)
    rows = xr_ref.shape[0]

    @pl.when(ti == 0)
    def _():
        h_ref[...] = jnp.zeros_like(h_ref)

    x = _causal_conv(xr_ref[...], xbuf_ref, cw_ref, cb_ref, ti == 0)
    ga, gx = [], []
    for n in range(nblocks):
        xb = x[:, n * bdim:(n + 1) * bdim].astype(BF16)
        ga.append(jnp.dot(xb, wa_ref[n].astype(BF16), preferred_element_type=F32))
        gx.append(jnp.dot(xb, wx_ref[n].astype(BF16), preferred_element_type=F32))
    gate_a = jnp.concatenate(ga, axis=-1) + ba_ref[...]
    gate_x = jnp.concatenate(gx, axis=-1) + bx_ref[...]
    log_a = -LRU_C * _sigmoid(gate_a) * _softplus(-lam_ref[...])
    a = jnp.exp(log_a)
    u = jnp.sqrt(1.0 - jnp.exp(2.0 * log_a)) * _sigmoid(gate_x) * x

    sub = jnp.bitwise_and(lax.broadcasted_iota(jnp.int32, a.shape, 0), SUBLANES - 1)
    s = 1
    while s < SUBLANES:
        keep = sub >= s
        a_prev = jnp.where(keep, pltpu.roll(a, s, 0), 1.0)
        u_prev = jnp.where(keep, pltpu.roll(u, s, 0), 0.0)
        u = a * u_prev + u
        a = a * a_prev
        s *= 2
    carry = h_ref[0:1, :]
    groups = []
    for gi in range(rows // SUBLANES):
        lo = gi * SUBLANES
        hg = u[lo:lo + SUBLANES] + a[lo:lo + SUBLANES] * carry
        groups.append(hg)
        carry = hg[SUBLANES - 1:SUBLANES, :]
    h = jnp.concatenate(groups, axis=0)
    h_ref[0:1, :] = carry

    g = gr_ref[...]
    gelu = 0.5 * g * (1.0 + jnp.tanh(math.sqrt(2.0 / math.pi) * (g + 0.044715 * g * g * g)))
    o_ref[...] = (h * gelu).astype(o_ref.dtype)


def conv_rglru(xg, conv_w, conv_b, w_a, b_a, w_x, b_x, lam, *, tt=256, tc=512):
    bsz, t, w2 = xg.shape
    w = w2 // 2
    nblocks_all, bdim, _ = w_a.shape
    nb = tc // bdim
    assert t % tt == 0 and w % tc == 0 and tc % bdim == 0
    row = lambda v: v.reshape(1, w)
    vec_spec = pl.BlockSpec((1, tc), lambda b, c, i: (0, c))
    gate_spec = pl.BlockSpec((nb, bdim, bdim), lambda b, c, i: (c, 0, 0))
    kern = functools.partial(_lru_kernel, nblocks=nb, bdim=bdim)
    return pl.pallas_call(
        kern,
        grid=(bsz, w // tc, t // tt),
        in_specs=[pl.BlockSpec((None, tt, tc), lambda b, c, i: (b, i, c)),
                  pl.BlockSpec((None, tt, tc), lambda b, c, i: (b, i, w // tc + c)),
                  pl.BlockSpec((CONV_WIDTH, tc), lambda b, c, i: (0, c)),
                  vec_spec, gate_spec, vec_spec, gate_spec, vec_spec, vec_spec],
        out_specs=pl.BlockSpec((None, tt, tc), lambda b, c, i: (b, i, c)),
        out_shape=jax.ShapeDtypeStruct((bsz, t, w), BF16),
        scratch_shapes=[pltpu.VMEM((CONV_PAD + tt, tc), F32),
                        pltpu.VMEM((8, tc), F32)],
        compiler_params=_params("parallel", "parallel", "arbitrary"),
        name="conv_rglru",
    )(xg, xg, conv_w, row(conv_b), w_a, row(b_a), w_x, row(b_x), row(lam))


def _gla_kernel(q_ref, k_ref, v_ref, r_ref, lr_ref, wa_ref, ba_ref, g_ref, o_ref, st_ref, *,
                chunk, scale, hps, dk, dv):
    ci = pl.program_id(2)

    @pl.when(ci == 0)
    def _():
        st_ref[...] = jnp.zeros_like(st_ref)

    z = jnp.dot(lr_ref[...].astype(BF16), wa_ref[...].astype(BF16),
                preferred_element_type=F32) + ba_ref[...]
    g = (jnp.minimum(z, 0.0) - jnp.log(1.0 + jnp.exp(-jnp.abs(z)))) / GLA_TAU
    row = lax.broadcasted_iota(jnp.int32, (chunk, chunk), 0)
    col = lax.broadcasted_iota(jnp.int32, (chunk, chunk), 1)
    sub0 = row - jnp.bitwise_and(row, GLA_SUB - 1)
    sums = jnp.concatenate([jnp.where(col <= row, 1.0, 0.0), jnp.where(col < sub0, 1.0, 0.0)],
                           axis=0).astype(BF16)
    bb = _dot_exact01(sums, g, ones_on_left=True)
    jrow = lax.broadcasted_iota(jnp.int32, (chunk, dk), 0)
    srow = lax.broadcasted_iota(jnp.int32, (GLA_SUB, chunk), 0)
    scol = lax.broadcasted_iota(jnp.int32, (GLA_SUB, chunk), 1)

    for h in range(hps):
        b = bb[0:chunk, h * dk:(h + 1) * dk]
        b_sub = bb[chunk:2 * chunk, h * dk:(h + 1) * dk]
        q = q_ref[:, h * dk:(h + 1) * dk].astype(F32) * scale
        k = k_ref[:, h * dk:(h + 1) * dk].astype(F32)
        v = v_ref[:, h * dv:(h + 1) * dv]
        b_last = b[chunk - 1:chunk, :]

        o = _nt((q * jnp.exp(b)).astype(BF16), st_ref[h].astype(BF16))

        parts = []
        for i in range(chunk // GLA_SUB):
            lo, hi = i * GLA_SUB, (i + 1) * GLA_SUB
            ref = b_sub[lo:lo + 1, :]
            qi = (q[lo:hi] * jnp.exp(b[lo:hi] - ref)).astype(BF16)
            ki = (k * jnp.exp(jnp.where(jrow < hi, ref - b, 0.0))).astype(BF16)
            att = _nt(qi, ki)
            att = jnp.where(scol <= srow + lo, att, 0.0)
            parts.append(jnp.dot(att.astype(BF16), v, preferred_element_type=F32))
        o = o + jnp.concatenate(parts, axis=0)

        kd = (k * jnp.exp(b_last - b)).astype(BF16)
        upd = lax.dot_general(v, kd, (((0,), (0,)), ((), ())), preferred_element_type=F32)
        st_ref[h] = st_ref[h] * jnp.exp(b_last) + upd

        y = o * lax.rsqrt(jnp.mean(o * o, axis=-1, keepdims=True) + NORM_EPS) * g_ref[...]
        gate = _silu(r_ref[:, h * dv:(h + 1) * dv].astype(F32))
        o_ref[:, h * dv:(h + 1) * dv] = (y * gate).astype(o_ref.dtype)


def gla(p1a, sm, w_a2p, b_a, norm, *, nh, dk, dv, chunk=128, hps=4):
    bsz, t, _ = p1a.shape
    hps = min(hps, nh)
    assert t % chunk == 0 and chunk % GLA_SUB == 0 and nh % hps == 0
    kq = nh * dk
    gk, gv = hps * dk, hps * dv
    koff, voff, roff = kq // gk, 2 * kq // gv, (2 * kq + nh * dv) // gv
    kern = functools.partial(_gla_kernel, chunk=chunk, scale=dk ** -0.5, hps=hps, dk=dk, dv=dv)
    return pl.pallas_call(
        kern,
        grid=(bsz, nh // hps, t // chunk),
        in_specs=[pl.BlockSpec((None, chunk, gk), lambda b, h, c: (b, c, h)),
                  pl.BlockSpec((None, chunk, gk), lambda b, h, c: (b, c, koff + h)),
                  pl.BlockSpec((None, chunk, gv), lambda b, h, c: (b, c, voff + h)),
                  pl.BlockSpec((None, chunk, gv), lambda b, h, c: (b, c, roff + h)),
                  pl.BlockSpec((None, chunk, LANES), lambda b, h, c: (b, c, 0)),
                  pl.BlockSpec((LANES, gk), lambda b, h, c: (0, h)),
                  pl.BlockSpec((1, gk), lambda b, h, c: (0, h)),
                  pl.BlockSpec((1, dv), lambda b, h, c: (0, 0))],
        out_specs=pl.BlockSpec((None, chunk, gv), lambda b, h, c: (b, c, h)),
        out_shape=jax.ShapeDtypeStruct((bsz, t, nh * dv), BF16),
        scratch_shapes=[pltpu.VMEM((hps, dv, dk), F32)],
        compiler_params=_params("parallel", "parallel", "arbitrary"),
        name="gla",
    )(p1a, p1a, p1a, p1a, sm, w_a2p, b_a.reshape(1, kq), norm.reshape(1, dv))


def _ssd_kernel(z_ref, xs_ref, bm_ref, cm_ref, dt_ref, cwx_ref, cwb_ref, cwc_ref, cbx_ref, cbb_ref,
                cbc_ref, dtb_ref, alog_ref, dfull_ref, norm_ref, o_ref,
                xbuf_ref, bbuf_ref, cbuf_ref, st_ref, *, chunk, hpg, hd):
    ci = pl.program_id(2)
    first = ci == 0

    @pl.when(first)
    def _():
        st_ref[...] = jnp.zeros_like(st_ref)

    xs = _silu(_causal_conv(xs_ref[...], xbuf_ref, cwx_ref, cbx_ref, first))
    bm = _silu(_causal_conv(bm_ref[...], bbuf_ref, cwb_ref, cbb_ref, first))
    cm = _silu(_causal_conv(cm_ref[...], cbuf_ref, cwc_ref, cbc_ref, first))
    width = hpg * hd

    dt = _softplus(dt_ref[...] + dtb_ref[...])
    da = dt * (-jnp.exp(alog_ref[...]))
    row = lax.broadcasted_iota(jnp.int32, (chunk, chunk), 0)
    col = lax.broadcasted_iota(jnp.int32, (chunk, chunk), 1)
    causal = col <= row
    tril = jnp.where(causal, 1.0, 0.0).astype(BF16)
    cs = _dot_exact01(tril, da, ones_on_left=True)
    cs_t = cs.T

    erow = lax.broadcasted_iota(jnp.int32, (LANES, width), 0)
    ecol = lax.broadcasted_iota(jnp.int32, (LANES, width), 1)
    expand = jnp.where((ecol >= erow * hd) & (ecol < (erow + 1) * hd), 1.0, 0.0).astype(BF16)
    full = _dot_exact01(expand, jnp.concatenate([cs, dt], axis=0), ones_on_left=False)
    cs_full = full[0:chunk]
    dt_full = full[chunk:2 * chunk]
    cs_last = cs_full[chunk - 1:chunk, :]

    xd = xs * dt_full
    xd_b = xd.astype(BF16)
    bm_b = bm.astype(BF16)
    cm_b = cm.astype(BF16)
    cb = _nt(cm_b, bm_b)

    lane = lax.broadcasted_iota(jnp.int32, (chunk, 2 * hd), 1)
    slabs = []
    for j in range(hpg // 2):
        xpair = xd_b[:, 2 * j * hd:(2 * j + 2) * hd]
        acc = None
        for half in range(2):
            h = 2 * j + half
            seg = cs[:, h:h + 1] - cs_t[h:h + 1, :]
            lmat = jnp.exp(jnp.where(causal, seg, NEG))
            mh = (cb * lmat).astype(BF16)
            mine = (lane >= half * hd) & (lane < (half + 1) * hd)
            part = jnp.dot(mh, jnp.where(mine, xpair, jnp.zeros_like(xpair)),
                           preferred_element_type=F32)
            acc = part if acc is None else acc + part
        slabs.append(acc)
    y = jnp.concatenate(slabs, axis=-1)

    st = st_ref[...]
    y = y + jnp.dot(cm_b, st.astype(BF16), preferred_element_type=F32) * jnp.exp(cs_full)
    xdd = (xd * jnp.exp(cs_last - cs_full)).astype(BF16)
    upd = lax.dot_general(bm_b, xdd, (((0,), (0,)), ((), ())), preferred_element_type=F32)
    st_ref[...] = st * jnp.exp(cs_last) + upd

    y = y + dfull_ref[...] * xs
    y = y * _silu(z_ref[...])
    y = y * lax.rsqrt(jnp.mean(y * y, axis=-1, keepdims=True) + NORM_EPS) * norm_ref[...]
    o_ref[...] = y.astype(o_ref.dtype)


def ssd(p1b, sm, conv_w, conv_b, dt_bias, a_log, d_skip, norm, *, nheads, hd, groups, nstate,
        chunk=256):
    bsz, t, _ = p1b.shape
    width = nheads * hd
    gw = width // groups
    hpg = nheads // groups
    assert t % chunk == 0 and nstate == LANES and gw % LANES == 0 and hpg % 2 == 0
    xoff = width // gw
    boff = 2 * width // nstate
    coff = boff + groups
    cxo, cbo, cco = 0, width // nstate, width // nstate + groups
    padl = lambda v: jnp.pad(v.reshape(groups, 1, hpg), ((0, 0), (0, 0), (0, LANES - hpg)))
    dfull = jnp.repeat(d_skip.reshape(groups, hpg), hd, axis=1).reshape(groups, 1, gw)
    cb2 = conv_b.reshape(1, -1)
    g3 = lambda b, g, c: (g, 0, 0)
    kern = functools.partial(_ssd_kernel, chunk=chunk, hpg=hpg, hd=hd)
    return pl.pallas_call(
        kern,
        grid=(bsz, groups, t // chunk),
        in_specs=[pl.BlockSpec((None, chunk, gw), lambda b, g, c: (b, c, g)),
                  pl.BlockSpec((None, chunk, gw), lambda b, g, c: (b, c, xoff + g)),
                  pl.BlockSpec((None, chunk, nstate), lambda b, g, c: (b, c, boff + g)),
                  pl.BlockSpec((None, chunk, nstate), lambda b, g, c: (b, c, coff + g)),
                  pl.BlockSpec((None, chunk, LANES), lambda b, g, c: (b, c, 1 + g)),
                  pl.BlockSpec((CONV_WIDTH, gw), lambda b, g, c: (0, cxo + g)),
                  pl.BlockSpec((CONV_WIDTH, nstate), lambda b, g, c: (0, cbo + g)),
                  pl.BlockSpec((CONV_WIDTH, nstate), lambda b, g, c: (0, cco + g)),
                  pl.BlockSpec((1, gw), lambda b, g, c: (0, cxo + g)),
                  pl.BlockSpec((1, nstate), lambda b, g, c: (0, cbo + g)),
                  pl.BlockSpec((1, nstate), lambda b, g, c: (0, cco + g)),
                  pl.BlockSpec((None, 1, LANES), g3),
                  pl.BlockSpec((None, 1, LANES), g3),
                  pl.BlockSpec((None, 1, gw), g3),
                  pl.BlockSpec((1, gw), lambda b, g, c: (0, g))],
        out_specs=pl.BlockSpec((None, chunk, gw), lambda b, g, c: (b, c, g)),
        out_shape=jax.ShapeDtypeStruct((bsz, t, width), BF16),
        scratch_shapes=[pltpu.VMEM((CONV_PAD + chunk, gw), F32),
                        pltpu.VMEM((CONV_PAD + chunk, nstate), F32),
                        pltpu.VMEM((CONV_PAD + chunk, nstate), F32),
                        pltpu.VMEM((nstate, gw), F32)],
        compiler_params=_params("parallel", "parallel", "arbitrary"),
        name="ssd",
    )(p1b, p1b, p1b, p1b, sm, conv_w, conv_w, conv_w, cb2, cb2, cb2,
      padl(dt_bias), padl(a_log), dfull, norm.reshape(1, width))


def _norm_router_kernel(x_ref, g_ref, r_ref, h_ref, route_ref, *, n_exp):
    x = x_ref[...]
    y = x * lax.rsqrt(jnp.mean(x * x, axis=-1, keepdims=True) + NORM_EPS) * g_ref[...]
    h_ref[...] = y
    logits = jnp.dot(y, r_ref[...], precision=HIGHEST, preferred_element_type=F32)
    lane = lax.broadcasted_iota(jnp.int32, logits.shape, 1)
    t = jnp.where(lane < n_exp, logits, -jnp.inf)
    m1 = jnp.max(t, axis=-1, keepdims=True)
    i1 = jnp.min(jnp.where(t == m1, lane, LANES), axis=-1, keepdims=True)
    t2 = jnp.where(lane == i1, -jnp.inf, t)
    m2 = jnp.max(t2, axis=-1, keepdims=True)
    i2 = jnp.min(jnp.where(t2 == m2, lane, LANES), axis=-1, keepdims=True)
    e = jnp.exp(m2 - m1)
    w1 = 1.0 / (1.0 + e)
    w2 = e / (1.0 + e)
    route = jnp.where(lane == 0, i1.astype(F32),
                      jnp.where(lane == 1, i2.astype(F32),
                                jnp.where(lane == 2, w1, jnp.where(lane == 3, w2, 0.0))))
    route_ref[...] = route


def norm_router(x, g, router, tm=256):
    m, d = x.shape
    n_exp = router.shape[1]
    rp = jnp.pad(router, ((0, 0), (0, LANES - n_exp)))
    return pl.pallas_call(
        functools.partial(_norm_router_kernel, n_exp=n_exp),
        grid=(m // tm,),
        in_specs=[pl.BlockSpec((tm, d), lambda i: (i, 0)),
                  pl.BlockSpec((1, d), lambda i: (0, 0)),
                  pl.BlockSpec((d, LANES), lambda i: (0, 0))],
        out_specs=[pl.BlockSpec((tm, d), lambda i: (i, 0)),
                   pl.BlockSpec((tm, LANES), lambda i: (i, 0))],
        out_shape=[jax.ShapeDtypeStruct((m, d), F32), jax.ShapeDtypeStruct((m, LANES), F32)],
        compiler_params=_params("parallel"),
        name="norm_router",
    )(x, g.reshape(1, d), rp)


def _row_gather(idx_ref, base, src_hbm, dst_ref, sem, rows):
    def start(i, c):
        for j in range(GATHER_UNROLL):
            r = i * GATHER_UNROLL + j
            pltpu.make_async_copy(src_hbm.at[pl.ds(idx_ref[base + r], 1)],
                                  dst_ref.at[pl.ds(r, 1)], sem).start(priority=j % 2)
        return c

    lax.fori_loop(0, rows // GATHER_UNROLL, start, 0)
    pltpu.make_async_copy(src_hbm.at[pl.ds(0, rows)], dst_ref, sem).wait()


def _dispatch_kernel(tok_ref, live_ref, h_hbm, o_ref, buf_ref, sem, *, sub):
    i = pl.program_id(0)
    rows = buf_ref.shape[0]
    live = live_ref[i // sub] - (i % sub) * rows

    @pl.when(live > 0)
    def _():
        _row_gather(tok_ref, i * rows, h_hbm, buf_ref, sem, rows)
        o_ref[...] = buf_ref[...].astype(o_ref.dtype)

    @pl.when(live <= 0)
    def _():
        o_ref[...] = jnp.zeros_like(o_ref)


def moe_dispatch(row_token, tile_rows, h, *, tile, tm=256):
    r = row_token.shape[0]
    d = h.shape[1]
    assert tm % GATHER_UNROLL == 0 and tile % tm == 0 and r % tile == 0
    return pl.pallas_call(
        functools.partial(_dispatch_kernel, sub=tile // tm),
        grid_spec=pltpu.PrefetchScalarGridSpec(
            num_scalar_prefetch=2,
            grid=(r // tm,),
            in_specs=[pl.BlockSpec(memory_space=pl.ANY)],
            out_specs=pl.BlockSpec((tm, d), lambda i, tok, live: (i, 0)),
            scratch_shapes=[pltpu.VMEM((tm, d), F32), pltpu.SemaphoreType.DMA(())]),
        out_shape=jax.ShapeDtypeStruct((r, d), BF16),
        compiler_params=_params("arbitrary"),
        name="moe_dispatch",
    )(row_token, tile_rows, h)


def _moe_mm_kernel(te_ref, tv_ref, x_ref, *refs, act_pair):
    n_w = 2 if act_pair else 1
    w_refs = refs[:n_w]
    o_ref = refs[n_w]
    r = pl.program_id(1)

    live = tv_ref[r]
    tm = x_ref.shape[0]
    half = tm // 2

    def compute(nrows):
        x = x_ref[0:nrows, :]
        out = jnp.dot(x, w_refs[0][...].astype(BF16), preferred_element_type=F32)
        if act_pair:
            out = _silu(out) * jnp.dot(x, w_refs[1][...].astype(BF16), preferred_element_type=F32)
        o_ref[0:nrows, :] = out.astype(o_ref.dtype)

    @pl.when(live > half)
    def _():
        compute(tm)

    @pl.when((live > 0) & (live <= half))
    def _():
        compute(half)
        o_ref[half:tm, :] = jnp.zeros((tm - half, o_ref.shape[1]), o_ref.dtype)

    @pl.when(live == 0)
    def _():
        o_ref[...] = jnp.zeros_like(o_ref)


def moe_matmul(tile_expert, tile_rows, x, w_list, *, out_dtype, tm=MOE_ROW_TILE, tn=256,
               name="moe_matmul"):
    r, k = x.shape
    n = w_list[0].shape[2]
    assert r % tm == 0 and n % tn == 0
    kern = functools.partial(_moe_mm_kernel, act_pair=len(w_list) == 2)
    w_spec = pl.BlockSpec((None, k, tn), lambda j, i, te, tv: (te[i], 0, j))
    return pl.pallas_call(
        kern,
        grid_spec=pltpu.PrefetchScalarGridSpec(
            num_scalar_prefetch=2,
            grid=(n // tn, r // tm),
            in_specs=[pl.BlockSpec((tm, k), lambda j, i, te, tv: (i, 0))] + [w_spec] * len(w_list),
            out_specs=pl.BlockSpec((tm, tn), lambda j, i, te, tv: (i, j))),
        out_shape=jax.ShapeDtypeStruct((r, n), out_dtype),
        compiler_params=_params("parallel", "arbitrary"),
        name=name,
    )(tile_expert, tile_rows, x, *w_list)


def _combine_kernel(d0_ref, d1_ref, x_ref, route_ref, g_ref, y_hbm, o_ref, buf0_ref, buf1_ref, sems):
    rows = x_ref.shape[0]
    base = pl.program_id(0) * rows
    _row_gather(d0_ref, base, y_hbm, buf0_ref, sems.at[0], rows)
    _row_gather(d1_ref, base, y_hbm, buf1_ref, sems.at[1], rows)
    w0 = route_ref[:, 2:3]
    w1 = route_ref[:, 3:4]
    x = x_ref[...] + w0 * buf0_ref[...] + w1 * buf1_ref[...]
    y = x * lax.rsqrt(jnp.mean(x * x, axis=-1, keepdims=True) + NORM_EPS)
    o_ref[...] = y * g_ref[...]


def moe_combine_norm(dest0, dest1, x, route, y, g, tm=256):
    m, d = x.shape
    assert tm % GATHER_UNROLL == 0
    return pl.pallas_call(
        _combine_kernel,
        grid_spec=pltpu.PrefetchScalarGridSpec(
            num_scalar_prefetch=2,
            grid=(m // tm,),
            in_specs=[pl.BlockSpec((tm, d), lambda i, a, b: (i, 0)),
                      pl.BlockSpec((tm, LANES), lambda i, a, b: (i, 0)),
                      pl.BlockSpec((1, d), lambda i, a, b: (0, 0)),
                      pl.BlockSpec(memory_space=pl.ANY)],
            out_specs=pl.BlockSpec((tm, d), lambda i, a, b: (i, 0)),
            scratch_shapes=[pltpu.VMEM((tm, d), F32), pltpu.VMEM((tm, d), F32),
                            pltpu.SemaphoreType.DMA((2,))]),
        out_shape=jax.ShapeDtypeStruct((m, d), F32),
        compiler_params=_params("arbitrary"),
        name="moe_combine_norm",
    )(dest0, dest1, x, route, g.reshape(1, d), y)


def _moe_plan(ids, tm):
    n_tok = ids.shape[0]
    e_flat = ids.reshape(-1)
    onehot = (e_flat[:, None] == jnp.arange(N_EXPERTS, dtype=jnp.int32)[None, :]).astype(jnp.int32)
    csum = jnp.cumsum(onehot, axis=0)
    pos = jnp.sum(csum * onehot, axis=1) - 1
    counts = csum[-1]
    padded = ((counts + tm - 1) // tm) * tm
    gend = jnp.cumsum(padded)
    gstart = gend - padded
    dest = gstart[e_flat] + pos
    n_rows = 2 * n_tok + N_EXPERTS * tm
    row_token = jnp.zeros((n_rows,), jnp.int32).at[dest].set(
        jnp.arange(2 * n_tok, dtype=jnp.int32) // 2)
    tile_start = jnp.arange(n_rows // tm, dtype=jnp.int32) * tm
    last_start = jnp.maximum(gend[-1] - tm, 0)
    probe = jnp.minimum(tile_start, last_start)
    tile_expert = jnp.minimum(jnp.sum((probe[:, None] >= gend[None, :]).astype(jnp.int32), axis=1),
                              N_EXPERTS - 1).astype(jnp.int32)
    live_end = (gstart + counts)[tile_expert]
    tile_rows = jnp.where(tile_start < gend[-1], jnp.clip(live_end - tile_start, 0, tm), 0)
    dest2 = dest.reshape(n_tok, 2).astype(jnp.int32)
    return row_token, tile_expert, tile_rows.astype(jnp.int32), dest2[:, 0], dest2[:, 1]


def kernel(x, ev_norm1, ev_w_in, ev_lru_conv_w, ev_lru_conv_b, ev_lru_w_a, ev_lru_b_a, ev_lru_w_x, ev_lru_b_x, ev_lru_lambda, ev_w_out, ev_norm2, ev_ffn_w1, ev_ffn_w3, ev_ffn_w2, od_norm1, od_w_in, od_gla_w_a2, od_gla_b_a, od_gla_norm, od_ssd_conv_w, od_ssd_conv_b, od_ssd_dt_bias, od_ssd_a_log, od_ssd_d, od_ssd_norm, od_w_out, od_norm2, od_router, od_moe_w1, od_moe_w3, od_moe_w2, final_norm):
    bsz, t, d = x.shape
    n_tok = bsz * t
    xf = x.reshape(n_tok, d)

    lru_w = ev_lru_lambda.shape[1]
    moba_w = (ev_w_in.shape[2] - 2 * lru_w) // 3
    moba_dh = LANES
    h = rmsnorm(xf, ev_norm1[0])
    qkv = matmul([h], [ev_w_in[0]], w_col0=0, n_cols=3 * moba_w, out_dtype=BF16, name="l0_in_qkv")
    xg = matmul([h], [ev_w_in[0]], w_col0=3 * moba_w, n_cols=2 * lru_w, out_dtype=F32, name="l0_in_lru")
    att = moba_attention(qkv.reshape(bsz, t, 3 * moba_w), nh=moba_w // moba_dh, dh=moba_dh)
    rec = conv_rglru(xg.reshape(bsz, t, 2 * lru_w), ev_lru_conv_w[0], ev_lru_conv_b[0], ev_lru_w_a[0],
                     ev_lru_b_a[0], ev_lru_w_x[0], ev_lru_b_x[0], ev_lru_lambda[0])
    xf = matmul([att.reshape(n_tok, moba_w), rec.reshape(n_tok, lru_w)], [ev_w_out[0]], res=xf,
                name="l0_out")
    h = rmsnorm(xf, ev_norm2[0])
    gact = swiglu_up(h, ev_ffn_w1[0], ev_ffn_w3[0], name="l0_ffn_up")
    xf = matmul_ktiled_res(gact, ev_ffn_w2[0], xf, name="l0_ffn_down")

    gla_key = od_gla_w_a2.shape[2]
    gla_rank = od_gla_w_a2.shape[1]
    gla_dv = od_gla_norm.shape[1]
    gla_heads = gla_key // (gla_dv // 2)
    gla_val = gla_heads * gla_dv
    ssd_heads = od_ssd_a_log.shape[1]
    ssd_width = od_ssd_norm.shape[1]
    ssd_conv_dim = od_ssd_conv_w.shape[2]
    ssd_groups = 4
    ssd_state = (ssd_conv_dim - ssd_width) // (2 * ssd_groups)
    hpg = ssd_heads // ssd_groups
    w_in1_t = od_w_in[0].T
    c_lr = 2 * gla_key + 2 * gla_val
    c_z = c_lr + gla_rank
    c_dt = c_z + ssd_width + ssd_conv_dim
    zpad = lambda n: jnp.zeros((n, d), F32)
    w_small_t = jnp.concatenate(
        [w_in1_t[c_lr:c_z], zpad(LANES - gla_rank)]
        + [blk for g in range(ssd_groups)
           for blk in (w_in1_t[c_dt + g * hpg:c_dt + (g + 1) * hpg], zpad(LANES - hpg))], axis=0)

    h = rmsnorm(xf, od_norm1[0])
    p1a = matmul_wt(h, w_in1_t, row0=0, n_cols=c_lr, out_dtype=BF16, name="l1_in_gla")
    p1b = matmul_wt(h, w_in1_t, row0=c_z, n_cols=c_dt - c_z, out_dtype=F32, name="l1_in_ssd")
    sm = matmul_wt(h, w_small_t, out_dtype=F32, tn=w_small_t.shape[0], name="l1_in_small")
    w_a2p = jnp.pad(od_gla_w_a2[0], ((0, LANES - gla_rank), (0, 0)))
    o_gla = gla(p1a.reshape(bsz, t, c_lr), sm.reshape(bsz, t, -1), w_a2p, od_gla_b_a[0], od_gla_norm[0],
                nh=gla_heads, dk=gla_key // gla_heads, dv=gla_dv)
    y_ssd = ssd(p1b.reshape(bsz, t, -1), sm.reshape(bsz, t, -1), od_ssd_conv_w[0], od_ssd_conv_b[0],
                od_ssd_dt_bias[0], od_ssd_a_log[0], od_ssd_d[0], od_ssd_norm[0],
                nheads=ssd_heads, hd=ssd_width // ssd_heads, groups=ssd_groups, nstate=ssd_state)
    xf = matmul([o_gla.reshape(n_tok, gla_val), y_ssd.reshape(n_tok, ssd_width)], [od_w_out[0]], res=xf,
                name="l1_out")

    hf, route = norm_router(xf, od_norm2[0], od_router[0])
    ids = route[:, 0:2].astype(jnp.int32)
    row_token, tile_expert, tile_rows, dest0, dest1 = _moe_plan(ids, MOE_ROW_TILE)
    xs = moe_dispatch(row_token, tile_rows, hf, tile=MOE_ROW_TILE)
    gact = moe_matmul(tile_expert, tile_rows, xs, [od_moe_w1[0], od_moe_w3[0]], out_dtype=BF16,
                      tn=MOE_COL_TILE, name="moe_up")
    ye = moe_matmul(tile_expert, tile_rows, gact, [od_moe_w2[0]], out_dtype=F32,
                    tn=2 * MOE_COL_TILE, name="moe_down")
    out = moe_combine_norm(dest0, dest1, xf, route, ye, final_norm)
    return out.reshape(bsz, t, d)
```

```python
import functools
import math

import jax
import jax.numpy as jnp
from jax import lax
from jax.experimental import pallas as pl
from jax.experimental.pallas import tpu as pltpu

F32 = jnp.float32
BF16 = jnp.bfloat16
HIGHEST = lax.Precision.HIGHEST

NORM_EPS = 1e-6
NEG = -1e30
LANES = 128
SUBLANES = 8
VMEM_LIMIT = 56 * 1024 * 1024

CONV_WIDTH = 4
CONV_PAD = 8
MOBA_BLOCK = 256
MOBA_TOPK = 3
LRU_C = 8.0
GLA_TAU = 16.0
GLA_SUB = 16
N_EXPERTS = 8
MOE_ROW_TILE = 512
MOE_COL_TILE = 512
GATHER_UNROLL = 8


def _nt(a, b, precision=None):
    return lax.dot_general(a, b, (((1,), (1,)), ((), ())), precision=precision,
                           preferred_element_type=F32)


def _dot_exact01(ones, x, *, ones_on_left):
    hi = x.astype(BF16)
    r1 = x - hi.astype(F32)
    mid = r1.astype(BF16)
    lo = (r1 - mid.astype(F32)).astype(BF16)
    out = None
    for term in (hi, mid, lo):
        part = (jnp.dot(ones, term, preferred_element_type=F32) if ones_on_left
                else jnp.dot(term, ones, preferred_element_type=F32))
        out = part if out is None else out + part
    return out


def _sigmoid(x):
    return 1.0 / (1.0 + jnp.exp(-x))


def _silu(x):
    return x * _sigmoid(x)


def _softplus(x):
    return jnp.maximum(x, 0.0) + jnp.log(1.0 + jnp.exp(-jnp.abs(x)))


def _params(*sem):
    return pltpu.CompilerParams(dimension_semantics=sem, vmem_limit_bytes=VMEM_LIMIT)


def _rmsnorm_kernel(x_ref, g_ref, o_ref):
    x = x_ref[...]
    y = x * lax.rsqrt(jnp.mean(x * x, axis=-1, keepdims=True) + NORM_EPS)
    o_ref[...] = (y * g_ref[...]).astype(o_ref.dtype)


def rmsnorm(x, g, out_dtype=BF16, tm=256):
    m, d = x.shape
    return pl.pallas_call(
        _rmsnorm_kernel,
        grid=(m // tm,),
        in_specs=[pl.BlockSpec((tm, d), lambda i: (i, 0)),
                  pl.BlockSpec((1, d), lambda i: (0, 0))],
        out_specs=pl.BlockSpec((tm, d), lambda i: (i, 0)),
        out_shape=jax.ShapeDtypeStruct((m, d), out_dtype),
        compiler_params=_params("parallel"),
        name="rmsnorm",
    )(x, g.reshape(1, d))


def _mm_kernel(*refs, n_a, has_res, act_pair):
    a_refs = refs[:n_a]
    n_w = 2 if act_pair else 1
    w_refs = refs[n_a:n_a + n_w]
    pos = n_a + n_w
    r_ref = refs[pos] if has_res else None
    pos += int(has_res)
    o_ref = refs[pos]
    wb_refs = refs[pos + 1:pos + 1 + n_w]

    @pl.when(pl.program_id(1) == 0)
    def _():
        for w_ref, wb_ref in zip(w_refs, wb_refs):
            wb_ref[...] = w_ref[...].astype(BF16)

    def contract(wb_ref):
        acc = None
        k0 = 0
        for a_ref in a_refs:
            ka = a_ref.shape[1]
            part = jnp.dot(a_ref[...], wb_ref[k0:k0 + ka, :], preferred_element_type=F32)
            acc = part if acc is None else acc + part
            k0 += ka
        return acc

    if act_pair:
        out = _silu(contract(wb_refs[0])) * contract(wb_refs[1])
    else:
        out = contract(wb_refs[0])
    if has_res:
        out = out + r_ref[...]
    o_ref[...] = out.astype(o_ref.dtype)


def matmul(a_list, w_list, *, w_col0=0, n_cols=None, res=None, out_dtype=F32, tm=1024, tn=512,
           name="matmul"):
    m = a_list[0].shape[0]
    k = sum(a.shape[1] for a in a_list)
    assert all(w.shape[0] == k for w in w_list)
    n_cols = w_list[0].shape[1] - w_col0 if n_cols is None else n_cols
    assert m % tm == 0 and n_cols % tn == 0 and w_col0 % tn == 0
    nb0 = w_col0 // tn
    in_specs = [pl.BlockSpec((tm, a.shape[1]), lambda n, i: (i, 0)) for a in a_list]
    in_specs += [pl.BlockSpec((k, tn), lambda n, i: (0, n + nb0)) for _ in w_list]
    args = list(a_list) + list(w_list)
    if res is not None:
        in_specs.append(pl.BlockSpec((tm, tn), lambda n, i: (i, n)))
        args.append(res)
    kern = functools.partial(_mm_kernel, n_a=len(a_list), has_res=res is not None,
                             act_pair=len(w_list) == 2)
    return pl.pallas_call(
        kern,
        grid=(n_cols // tn, m // tm),
        in_specs=in_specs,
        out_specs=pl.BlockSpec((tm, tn), lambda n, i: (i, n)),
        out_shape=jax.ShapeDtypeStruct((m, n_cols), out_dtype),
        scratch_shapes=[pltpu.VMEM((k, tn), BF16) for _ in w_list],
        compiler_params=_params("parallel", "arbitrary"),
        name=name,
    )(*args)


def _mm_wt_kernel(a_ref, w_ref, *rest, row_shift):
    if row_shift:
        wn_ref, o_ref, wb_ref = rest
    else:
        o_ref, wb_ref = rest

    @pl.when(pl.program_id(1) == 0)
    def _():
        tn = w_ref.shape[0]
        if row_shift:
            wb_ref[0:tn - row_shift, :] = w_ref[row_shift:tn, :].astype(BF16)
            wb_ref[tn - row_shift:tn, :] = wn_ref[...].astype(BF16)
        else:
            wb_ref[...] = w_ref[...].astype(BF16)

    o_ref[...] = _nt(a_ref[...], wb_ref[...]).astype(o_ref.dtype)


def matmul_wt(a, wt, *, row0=0, n_cols=None, out_dtype=F32, tm=1024, tn=512, name="matmul_wt"):
    m, k = a.shape
    assert wt.shape[1] == k
    n_cols = wt.shape[0] - row0 if n_cols is None else n_cols
    row_shift = row0 % tn
    nb0 = (row0 - row_shift) // tn
    assert m % tm == 0 and n_cols % tn == 0 and row_shift % 16 == 0 and (not row_shift or tn % row_shift == 0)
    in_specs = [pl.BlockSpec((tm, k), lambda n, i: (i, 0)),
                pl.BlockSpec((tn, k), lambda n, i: (n + nb0, 0))]
    args = [a, wt]
    if row_shift:
        in_specs.append(pl.BlockSpec((row_shift, k), lambda n, i: ((n + nb0 + 1) * (tn // row_shift), 0)))
        args.append(wt)
    return pl.pallas_call(
        functools.partial(_mm_wt_kernel, row_shift=row_shift),
        grid=(n_cols // tn, m // tm),
        in_specs=in_specs,
        out_specs=pl.BlockSpec((tm, tn), lambda n, i: (i, n)),
        out_shape=jax.ShapeDtypeStruct((m, n_cols), out_dtype),
        scratch_shapes=[pltpu.VMEM((tn, k), BF16)],
        compiler_params=_params("parallel", "arbitrary"),
        name=name,
    )(*args)


def _mm_rows_kernel(a_ref, w1_ref, w3_ref, o_ref):
    a = a_ref[...]
    gate = jnp.dot(a, w1_ref[...].astype(BF16), preferred_element_type=F32)
    up = jnp.dot(a, w3_ref[...].astype(BF16), preferred_element_type=F32)
    o_ref[...] = (_silu(gate) * up).astype(o_ref.dtype)


def swiglu_up(a, w1, w3, *, tm=2048, tn=256, name="swiglu_up"):
    m, k = a.shape
    n = w1.shape[1]
    assert m % tm == 0 and n % tn == 0
    w_spec = pl.BlockSpec((k, tn), lambda i, j: (0, j))
    return pl.pallas_call(
        _mm_rows_kernel,
        grid=(m // tm, n // tn),
        in_specs=[pl.BlockSpec((tm, k), lambda i, j: (i, 0), pipeline_mode=pl.Buffered(1)),
                  w_spec, w_spec],
        out_specs=pl.BlockSpec((tm, tn), lambda i, j: (i, j)),
        out_shape=jax.ShapeDtypeStruct((m, n), BF16),
        compiler_params=_params("parallel", "arbitrary"),
        name=name,
    )(a, w1, w3)


def _mm_ktiled_kernel(a_ref, w_ref, r_ref, o_ref):
    @pl.when(pl.program_id(2) == 0)
    def _():
        o_ref[...] = r_ref[...]

    o_ref[...] += jnp.dot(a_ref[...], w_ref[...].astype(BF16), preferred_element_type=F32)


def matmul_ktiled_res(a, w, res, *, tm=1024, tn=1024, tk=2048, name="matmul_ktiled"):
    m, k = a.shape
    n = w.shape[1]
    assert m % tm == 0 and n % tn == 0 and k % tk == 0
    return pl.pallas_call(
        _mm_ktiled_kernel,
        grid=(m // tm, n // tn, k // tk),
        in_specs=[pl.BlockSpec((tm, tk), lambda i, j, kk: (i, kk)),
                  pl.BlockSpec((tk, tn), lambda i, j, kk: (kk, j)),
                  pl.BlockSpec((tm, tn), lambda i, j, kk: (i, j))],
        out_specs=pl.BlockSpec((tm, tn), lambda i, j, kk: (i, j)),
        out_shape=jax.ShapeDtypeStruct((m, n), F32),
        compiler_params=_params("parallel", "parallel", "arbitrary"),
        name=name,
    )(a, w, res)


def _moba_kernel(q_ref, k_ref, v_ref, o_ref, kmean_ref, sel_ref, *, blk, nblk, topk, scale, span):
    qb = pl.program_id(2)
    nbp = kmean_ref.shape[0]

    @pl.when(qb == 0)
    def _():
        kmean_ref[...] = jnp.zeros_like(kmean_ref)
        for n in range(nblk):
            kb = k_ref[n * blk:(n + 1) * blk, :].astype(F32)
            kmean_ref[n:n + 1, :] = jnp.mean(kb, axis=0, keepdims=True)

    qs = q_ref[...].astype(F32) * scale
    gate = _nt(kmean_ref[...], qs, precision=HIGHEST)
    bid = lax.broadcasted_iota(jnp.int32, gate.shape, 0)
    t = jnp.where(bid < qb, gate, -jnp.inf)
    sel = jnp.zeros(gate.shape, F32)
    for _ in range(topk):
        mx = jnp.max(t, axis=0, keepdims=True)
        idx = jnp.min(jnp.where(t == mx, bid, nbp), axis=0, keepdims=True)
        pick = (bid == idx) & (mx > -jnp.inf)
        sel = jnp.where(pick, 1.0, sel)
        t = jnp.where(bid == idx, -jnp.inf, t)
    sel_ref[...] = jnp.concatenate([sel, jnp.zeros((LANES - nbp, blk), F32)], axis=0).T

    qsb = qs.astype(BF16)
    row = lax.broadcasted_iota(jnp.int32, (blk, blk), 0)
    col = lax.broadcasted_iota(jnp.int32, (blk, blk), 1)
    ahead = col - row
    for nv in range(span, nblk + 1, span):
        @pl.when((qb >= nv - span) & (qb < nv))
        def _(nv=nv):
            s = _nt(qsb, k_ref[0:nv * blk, :])
            pieces = []
            for kb in range(nv):
                sb = s[:, kb * blk:(kb + 1) * blk]
                if kb >= nv - span:
                    own_limit = jnp.where(qb == kb, 0, -2 * blk)
                    piece = jnp.where(ahead <= own_limit, sb, NEG)
                else:
                    piece = NEG
                pieces.append(jnp.where(sel_ref[:, kb:kb + 1] > 0.5, sb, piece))
            m = jnp.max(functools.reduce(jnp.maximum, pieces), axis=-1, keepdims=True)
            probs = [jnp.exp(piece - m) for piece in pieces]
            l = jnp.sum(functools.reduce(jnp.add, probs), axis=-1, keepdims=True)
            p = jnp.concatenate([pr.astype(BF16) for pr in probs], axis=1)
            o = jnp.dot(p, v_ref[0:nv * blk, :], preferred_element_type=F32)
            o_ref[...] = (o / l).astype(o_ref.dtype)


def moba_attention(qkv, *, nh, dh, span=2):
    bsz, t, _ = qkv.shape
    blk = MOBA_BLOCK
    nblk = t // blk
    nbp = -(-nblk // 8) * 8
    assert t % blk == 0 and nbp <= LANES and dh % LANES == 0 and nblk % span == 0
    kern = functools.partial(_moba_kernel, blk=blk, nblk=nblk, topk=min(MOBA_TOPK, nblk),
                             scale=dh ** -0.5, span=span)
    return pl.pallas_call(
        kern,
        grid=(bsz, nh, nblk),
        in_specs=[pl.BlockSpec((None, blk, dh), lambda b, h, i: (b, i, h)),
                  pl.BlockSpec((None, t, dh), lambda b, h, i: (b, 0, nh + h)),
                  pl.BlockSpec((None, t, dh), lambda b, h, i: (b, 0, 2 * nh + h))],
        out_specs=pl.BlockSpec((None, blk, dh), lambda b, h, i: (b, i, h)),
        out_shape=jax.ShapeDtypeStruct((bsz, t, nh * dh), BF16),
        scratch_shapes=[pltpu.VMEM((nbp, dh), F32),
                        pltpu.VMEM((blk, LANES), F32)],
        compiler_params=_params("parallel", "parallel", "arbitrary"),
        name="moba_attention",
    )(qkv, qkv, qkv)


def _causal_conv(x, buf_ref, w_ref, b_ref, first):
    rows = x.shape[0]

    @pl.when(first)
    def _():
        buf_ref[0:CONV_PAD, :] = jnp.zeros((CONV_PAD, x.shape[1]), F32)

    buf_ref[CONV_PAD:CONV_PAD + rows, :] = x
    y = b_ref[...] + w_ref[CONV_WIDTH - 1:CONV_WIDTH, :] * x
    for kk in range(CONV_WIDTH - 1):
        off = CONV_PAD - (CONV_WIDTH - 1) + kk
        y = y + w_ref[kk:kk + 1, :] * buf_ref[off:off + rows, :]
    buf_ref[0:CONV_PAD, :] = x[rows - CONV_PAD:rows, :]
    return y


def _lru_kernel(xr_ref, gr_ref, cw_ref, cb_ref, wa_ref, ba_ref, wx_ref, bx_ref, lam_ref, o_ref,
                xbuf_ref, h_ref, *, nblocks, bdim):
    ti = pl.program_id(2)
    rows = xr_ref.shape[0]

    @pl.when(ti == 0)
    def _():
        h_ref[...] = jnp.zeros_like(h_ref)

    x = _causal_conv(xr_ref[...], xbuf_ref, cw_ref, cb_ref, ti == 0)
    ga, gx = [], []
    for n in range(nblocks):
        xb = x[:, n * bdim:(n + 1) * bdim].astype(BF16)
        ga.append(jnp.dot(xb, wa_ref[n].astype(BF16), preferred_element_type=F32))
        gx.append(jnp.dot(xb, wx_ref[n].astype(BF16), preferred_element_type=F32))
    gate_a = jnp.concatenate(ga, axis=-1) + ba_ref[...]
    gate_x = jnp.concatenate(gx, axis=-1) + bx_ref[...]
    log_a = -LRU_C * _sigmoid(gate_a) * _softplus(-lam_ref[...])
    a = jnp.exp(log_a)
    u = jnp.sqrt(1.0 - jnp.exp(2.0 * log_a)) * _sigmoid(gate_x) * x

    sub = jnp.bitwise_and(lax.broadcasted_iota(jnp.int32, a.shape, 0), SUBLANES - 1)
    s = 1
    while s < SUBLANES:
        keep = sub >= s
        a_prev = jnp.where(keep, pltpu.roll(a, s, 0), 1.0)
        u_prev = jnp.where(keep, pltpu.roll(u, s, 0), 0.0)
        u = a * u_prev + u
        a = a * a_prev
        s *= 2
    carry = h_ref[0:1, :]
    groups = []
    for gi in range(rows // SUBLANES):
        lo = gi * SUBLANES
        hg = u[lo:lo + SUBLANES] + a[lo:lo + SUBLANES] * carry
        groups.append(hg)
        carry = hg[SUBLANES - 1:SUBLANES, :]
    h = jnp.concatenate(groups, axis=0)
    h_ref[0:1, :] = carry

    g = gr_ref[...]
    gelu = 0.5 * g * (1.0 + jnp.tanh(math.sqrt(2.0 / math.pi) * (g + 0.044715 * g * g * g)))
    o_ref[...] = (h * gelu).astype(o_ref.dtype)


def conv_rglru(xg, conv_w, conv_b, w_a, b_a, w_x, b_x, lam, *, tt=256, tc=512):
    bsz, t, w2 = xg.shape
    w = w2 // 2
    nblocks_all, bdim, _ = w_a.shape
    nb = tc // bdim
    assert t % tt == 0 and w % tc == 0 and tc % bdim == 0
    row = lambda v: v.reshape(1, w)
    vec_spec = pl.BlockSpec((1, tc), lambda b, c, i: (0, c))
    gate_spec = pl.BlockSpec((nb, bdim, bdim), lambda b, c, i: (c, 0, 0))
    kern = functools.partial(_lru_kernel, nblocks=nb, bdim=bdim)
    return pl.pallas_call(
        kern,
        grid=(bsz, w // tc, t // tt),
        in_specs=[pl.BlockSpec((None, tt, tc), lambda b, c, i: (b, i, c)),
                  pl.BlockSpec((None, tt, tc), lambda b, c, i: (b, i, w // tc + c)),
                  pl.BlockSpec((CONV_WIDTH, tc), lambda b, c, i: (0, c)),
                  vec_spec, gate_spec, vec_spec, gate_spec, vec_spec, vec_spec],
        out_specs=pl.BlockSpec((None, tt, tc), lambda b, c, i: (b, i, c)),
        out_shape=jax.ShapeDtypeStruct((bsz, t, w), BF16),
        scratch_shapes=[pltpu.VMEM((CONV_PAD + tt, tc), F32),
                        pltpu.VMEM((8, tc), F32)],
        compiler_params=_params("parallel", "parallel", "arbitrary"),
        name="conv_rglru",
    )(xg, xg, conv_w, row(conv_b), w_a, row(b_a), w_x, row(b_x), row(lam))


def _gla_kernel(q_ref, k_ref, v_ref, r_ref, lr_ref, wa_ref, ba_ref, g_ref, o_ref, st_ref, *,
                chunk, scale, hps, dk, dv):
    ci = pl.program_id(2)

    @pl.when(ci == 0)
    def _():
        st_ref[...] = jnp.zeros_like(st_ref)

    z = jnp.dot(lr_ref[...].astype(BF16), wa_ref[...].astype(BF16),
                preferred_element_type=F32) + ba_ref[...]
    g = (jnp.minimum(z, 0.0) - jnp.log(1.0 + jnp.exp(-jnp.abs(z)))) / GLA_TAU
    row = lax.broadcasted_iota(jnp.int32, (chunk, chunk), 0)
    col = lax.broadcasted_iota(jnp.int32, (chunk, chunk), 1)
    sub0 = row - jnp.bitwise_and(row, GLA_SUB - 1)
    sums = jnp.concatenate([jnp.where(col <= row, 1.0, 0.0), jnp.where(col < sub0, 1.0, 0.0)],
                           axis=0).astype(BF16)
    bb = _dot_exact01(sums, g, ones_on_left=True)
    jrow = lax.broadcasted_iota(jnp.int32, (chunk, dk), 0)
    srow = lax.broadcasted_iota(jnp.int32, (GLA_SUB, chunk), 0)
    scol = lax.broadcasted_iota(jnp.int32, (GLA_SUB, chunk), 1)

    for h in range(hps):
        b = bb[0:chunk, h * dk:(h + 1) * dk]
        b_sub = bb[chunk:2 * chunk, h * dk:(h + 1) * dk]
        q = q_ref[:, h * dk:(h + 1) * dk].astype(F32) * scale
        k = k_ref[:, h * dk:(h + 1) * dk].astype(F32)
        v = v_ref[:, h * dv:(h + 1) * dv]
        b_last = b[chunk - 1:chunk, :]

        o = _nt((q * jnp.exp(b)).astype(BF16), st_ref[h].astype(BF16))

        parts = []
        for i in range(chunk // GLA_SUB):
            lo, hi = i * GLA_SUB, (i + 1) * GLA_SUB
            ref = b_sub[lo:lo + 1, :]
            qi = (q[lo:hi] * jnp.exp(b[lo:hi] - ref)).astype(BF16)
            ki = (k * jnp.exp(jnp.where(jrow < hi, ref - b, 0.0))).astype(BF16)
            att = _nt(qi, ki)
            parts.append(jnp.where(scol <= srow + lo, att, 0.0).astype(BF16))
        o = o + jnp.dot(jnp.concatenate(parts, axis=0), v, preferred_element_type=F32)

        kd = (k * jnp.exp(b_last - b)).astype(BF16)
        upd = lax.dot_general(v, kd, (((0,), (0,)), ((), ())), preferred_element_type=F32)
        st_ref[h] = st_ref[h] * jnp.exp(b_last) + upd

        y = o * lax.rsqrt(jnp.mean(o * o, axis=-1, keepdims=True) + NORM_EPS) * g_ref[...]
        gate = _silu(r_ref[:, h * dv:(h + 1) * dv].astype(F32))
        o_ref[:, h * dv:(h + 1) * dv] = (y * gate).astype(o_ref.dtype)


def gla(p1a, sm, w_a2p, b_a, norm, *, nh, dk, dv, chunk=128, hps=4):
    bsz, t, _ = p1a.shape
    hps = min(hps, nh)
    assert t % chunk == 0 and chunk % GLA_SUB == 0 and nh % hps == 0
    kq = nh * dk
    gk, gv = hps * dk, hps * dv
    koff, voff, roff = kq // gk, 2 * kq // gv, (2 * kq + nh * dv) // gv
    kern = functools.partial(_gla_kernel, chunk=chunk, scale=dk ** -0.5, hps=hps, dk=dk, dv=dv)
    return pl.pallas_call(
        kern,
        grid=(bsz, nh // hps, t // chunk),
        in_specs=[pl.BlockSpec((None, chunk, gk), lambda b, h, c: (b, c, h)),
                  pl.BlockSpec((None, chunk, gk), lambda b, h, c: (b, c, koff + h)),
                  pl.BlockSpec((None, chunk, gv), lambda b, h, c: (b, c, voff + h)),
                  pl.BlockSpec((None, chunk, gv), lambda b, h, c: (b, c, roff + h)),
                  pl.BlockSpec((None, chunk, LANES), lambda b, h, c: (b, c, 0)),
                  pl.BlockSpec((LANES, gk), lambda b, h, c: (0, h)),
                  pl.BlockSpec((1, gk), lambda b, h, c: (0, h)),
                  pl.BlockSpec((1, dv), lambda b, h, c: (0, 0))],
        out_specs=pl.BlockSpec((None, chunk, gv), lambda b, h, c: (b, c, h)),
        out_shape=jax.ShapeDtypeStruct((bsz, t, nh * dv), BF16),
        scratch_shapes=[pltpu.VMEM((hps, dv, dk), F32)],
        compiler_params=_params("parallel", "parallel", "arbitrary"),
        name="gla",
    )(p1a, p1a, p1a, p1a, sm, w_a2p, b_a.reshape(1, kq), norm.reshape(1, dv))


def _ssd_kernel(z_ref, xs_ref, bm_ref, cm_ref, dt_ref, cwx_ref, cwb_ref, cwc_ref, cbx_ref, cbb_ref,
                cbc_ref, dtb_ref, alog_ref, dfull_ref, norm_ref, o_ref,
                xbuf_ref, bbuf_ref, cbuf_ref, st_ref, *, chunk, hpg, hd):
    ci = pl.program_id(2)
    first = ci == 0

    @pl.when(first)
    def _():
        st_ref[...] = jnp.zeros_like(st_ref)

    xs = _silu(_causal_conv(xs_ref[...], xbuf_ref, cwx_ref, cbx_ref, first))
    bm = _silu(_causal_conv(bm_ref[...], bbuf_ref, cwb_ref, cbb_ref, first))
    cm = _silu(_causal_conv(cm_ref[...], cbuf_ref, cwc_ref, cbc_ref, first))
    width = hpg * hd

    dt = _softplus(dt_ref[...] + dtb_ref[...])
    da = dt * (-jnp.exp(alog_ref[...]))
    row = lax.broadcasted_iota(jnp.int32, (chunk, chunk), 0)
    col = lax.broadcasted_iota(jnp.int32, (chunk, chunk), 1)
    causal = col <= row
    tril = jnp.where(causal, 1.0, 0.0).astype(BF16)
    cs = _dot_exact01(tril, da, ones_on_left=True)
    cs_t = cs.T

    erow = lax.broadcasted_iota(jnp.int32, (LANES, width), 0)
    ecol = lax.broadcasted_iota(jnp.int32, (LANES, width), 1)
    expand = jnp.where((ecol >= erow * hd) & (ecol < (erow + 1) * hd), 1.0, 0.0).astype(BF16)
    full = _dot_exact01(expand, jnp.concatenate([cs, dt], axis=0), ones_on_left=False)
    cs_full = full[0:chunk]
    dt_full = full[chunk:2 * chunk]
    cs_last = cs_full[chunk - 1:chunk, :]

    xd = xs * dt_full
    xd_b = xd.astype(BF16)
    bm_b = bm.astype(BF16)
    cm_b = cm.astype(BF16)
    cb = _nt(cm_b, bm_b)

    lane = lax.broadcasted_iota(jnp.int32, (chunk, 2 * hd), 1)
    slabs = []
    for j in range(hpg // 2):
        xpair = xd_b[:, 2 * j * hd:(2 * j + 2) * hd]
        acc = None
        for half in range(2):
            h = 2 * j + half
            seg = cs[:, h:h + 1] - cs_t[h:h + 1, :]
            lmat = jnp.exp(jnp.where(causal, seg, NEG))
            mh = (cb * lmat).astype(BF16)
            mine = (lane >= half * hd) & (lane < (half + 1) * hd)
            part = jnp.dot(mh, jnp.where(mine, xpair, jnp.zeros_like(xpair)),
                           preferred_element_type=F32)
            acc = part if acc is None else acc + part
        slabs.append(acc)
    y = jnp.concatenate(slabs, axis=-1)

    st = st_ref[...]
    y = y + jnp.dot(cm_b, st.astype(BF16), preferred_element_type=F32) * jnp.exp(cs_full)
    xdd = (xd * jnp.exp(cs_last - cs_full)).astype(BF16)
    upd = lax.dot_general(bm_b, xdd, (((0,), (0,)), ((), ())), preferred_element_type=F32)
    st_ref[...] = st * jnp.exp(cs_last) + upd

    y = y + dfull_ref[...] * xs
    y = y * _silu(z_ref[...])
    y = y * lax.rsqrt(jnp.mean(y * y, axis=-1, keepdims=True) + NORM_EPS) * norm_ref[...]
    o_ref[...] = y.astype(o_ref.dtype)


def ssd(p1b, sm, conv_w, conv_b, dt_bias, a_log, d_skip, norm, *, nheads, hd, groups, nstate,
        chunk=256):
    bsz, t, _ = p1b.shape
    width = nheads * hd
    gw = width // groups
    hpg = nheads // groups
    assert t % chunk == 0 and nstate == LANES and gw % LANES == 0 and hpg % 2 == 0
    xoff = width // gw
    boff = 2 * width // nstate
    coff = boff + groups
    cxo, cbo, cco = 0, width // nstate, width // nstate + groups
    padl = lambda v: jnp.pad(v.reshape(groups, 1, hpg), ((0, 0), (0, 0), (0, LANES - hpg)))
    dfull = jnp.repeat(d_skip.reshape(groups, hpg), hd, axis=1).reshape(groups, 1, gw)
    cb2 = conv_b.reshape(1, -1)
    g3 = lambda b, g, c: (g, 0, 0)
    kern = functools.partial(_ssd_kernel, chunk=chunk, hpg=hpg, hd=hd)
    return pl.pallas_call(
        kern,
        grid=(bsz, groups, t // chunk),
        in_specs=[pl.BlockSpec((None, chunk, gw), lambda b, g, c: (b, c, g)),
                  pl.BlockSpec((None, chunk, gw), lambda b, g, c: (b, c, xoff + g)),
                  pl.BlockSpec((None, chunk, nstate), lambda b, g, c: (b, c, boff + g)),
                  pl.BlockSpec((None, chunk, nstate), lambda b, g, c: (b, c, coff + g)),
                  pl.BlockSpec((None, chunk, LANES), lambda b, g, c: (b, c, 1 + g)),
                  pl.BlockSpec((CONV_WIDTH, gw), lambda b, g, c: (0, cxo + g)),
                  pl.BlockSpec((CONV_WIDTH, nstate), lambda b, g, c: (0, cbo + g)),
                  pl.BlockSpec((CONV_WIDTH, nstate), lambda b, g, c: (0, cco + g)),
                  pl.BlockSpec((1, gw), lambda b, g, c: (0, cxo + g)),
                  pl.BlockSpec((1, nstate), lambda b, g, c: (0, cbo + g)),
                  pl.BlockSpec((1, nstate), lambda b, g, c: (0, cco + g)),
                  pl.BlockSpec((None, 1, LANES), g3),
                  pl.BlockSpec((None, 1, LANES), g3),
                  pl.BlockSpec((None, 1, gw), g3),
                  pl.BlockSpec((1, gw), lambda b, g, c: (0, g))],
        out_specs=pl.BlockSpec((None, chunk, gw), lambda b, g, c: (b, c, g)),
        out_shape=jax.ShapeDtypeStruct((bsz, t, width), BF16),
        scratch_shapes=[pltpu.VMEM((CONV_PAD + chunk, gw), F32),
                        pltpu.VMEM((CONV_PAD + chunk, nstate), F32),
                        pltpu.VMEM((CONV_PAD + chunk, nstate), F32),
                        pltpu.VMEM((nstate, gw), F32)],
        compiler_params=_params("parallel", "parallel", "arbitrary"),
        name="ssd",
    )(p1b, p1b, p1b, p1b, sm, conv_w, conv_w, conv_w, cb2, cb2, cb2,
      padl(dt_bias), padl(a_log), dfull, norm.reshape(1, width))


def _norm_router_kernel(x_ref, g_ref, r_ref, h_ref, route_ref, *, n_exp):
    x = x_ref[...]
    y = x * lax.rsqrt(jnp.mean(x * x, axis=-1, keepdims=True) + NORM_EPS) * g_ref[...]
    h_ref[...] = y
    logits = jnp.dot(y, r_ref[...], precision=HIGHEST, preferred_element_type=F32)
    lane = lax.broadcasted_iota(jnp.int32, logits.shape, 1)
    t = jnp.where(lane < n_exp, logits, -jnp.inf)
    m1 = jnp.max(t, axis=-1, keepdims=True)
    i1 = jnp.min(jnp.where(t == m1, lane, LANES), axis=-1, keepdims=True)
    t2 = jnp.where(lane == i1, -jnp.inf, t)
    m2 = jnp.max(t2, axis=-1, keepdims=True)
    i2 = jnp.min(jnp.where(t2 == m2, lane, LANES), axis=-1, keepdims=True)
    e = jnp.exp(m2 - m1)
    w1 = 1.0 / (1.0 + e)
    w2 = e / (1.0 + e)
    route = jnp.where(lane == 0, i1.astype(F32),
                      jnp.where(lane == 1, i2.astype(F32),
                                jnp.where(lane == 2, w1, jnp.where(lane == 3, w2, 0.0))))
    route_ref[...] = route


def norm_router(x, g, router, tm=256):
    m, d = x.shape
    n_exp = router.shape[1]
    rp = jnp.pad(router, ((0, 0), (0, LANES - n_exp)))
    return pl.pallas_call(
        functools.partial(_norm_router_kernel, n_exp=n_exp),
        grid=(m // tm,),
        in_specs=[pl.BlockSpec((tm, d), lambda i: (i, 0)),
                  pl.BlockSpec((1, d), lambda i: (0, 0)),
                  pl.BlockSpec((d, LANES), lambda i: (0, 0))],
        out_specs=[pl.BlockSpec((tm, d), lambda i: (i, 0)),
                   pl.BlockSpec((tm, LANES), lambda i: (i, 0))],
        out_shape=[jax.ShapeDtypeStruct((m, d), F32), jax.ShapeDtypeStruct((m, LANES), F32)],
        compiler_params=_params("parallel"),
        name="norm_router",
    )(x, g.reshape(1, d), rp)


def _row_gather(idx_ref, base, src_hbm, dst_ref, sem, rows):
    def start(i, c):
        for j in range(GATHER_UNROLL):
            r = i * GATHER_UNROLL + j
            pltpu.make_async_copy(src_hbm.at[pl.ds(idx_ref[base + r], 1)],
                                  dst_ref.at[pl.ds(r, 1)], sem).start(priority=j % 2)
        return c

    lax.fori_loop(0, rows // GATHER_UNROLL, start, 0)
    pltpu.make_async_copy(src_hbm.at[pl.ds(0, rows)], dst_ref, sem).wait()


def _dispatch_kernel(tok_ref, live_ref, h_hbm, o_ref, buf_ref, sem, *, sub):
    i = pl.program_id(0)
    rows = buf_ref.shape[0]
    live = live_ref[i // sub] - (i % sub) * rows

    @pl.when(live > 0)
    def _():
        _row_gather(tok_ref, i * rows, h_hbm, buf_ref, sem, rows)
        o_ref[...] = buf_ref[...].astype(o_ref.dtype)

    @pl.when(live <= 0)
    def _():
        o_ref[...] = jnp.zeros_like(o_ref)


def moe_dispatch(row_token, tile_rows, h, *, tile, tm=256):
    r = row_token.shape[0]
    d = h.shape[1]
    assert tm % GATHER_UNROLL == 0 and tile % tm == 0 and r % tile == 0
    return pl.pallas_call(
        functools.partial(_dispatch_kernel, sub=tile // tm),
        grid_spec=pltpu.PrefetchScalarGridSpec(
            num_scalar_prefetch=2,
            grid=(r // tm,),
            in_specs=[pl.BlockSpec(memory_space=pl.ANY)],
            out_specs=pl.BlockSpec((tm, d), lambda i, tok, live: (i, 0)),
            scratch_shapes=[pltpu.VMEM((tm, d), F32), pltpu.SemaphoreType.DMA(())]),
        out_shape=jax.ShapeDtypeStruct((r, d), BF16),
        compiler_params=_params("arbitrary"),
        name="moe_dispatch",
    )(row_token, tile_rows, h)


def _moe_mm_kernel(te_ref, tv_ref, x_ref, *refs, act_pair):
    n_w = 2 if act_pair else 1
    w_refs = refs[:n_w]
    o_ref = refs[n_w]
    r = pl.program_id(1)

    live = tv_ref[r]
    tm = x_ref.shape[0]
    half = tm // 2

    def compute(nrows):
        x = x_ref[0:nrows, :]
        out = jnp.dot(x, w_refs[0][...].astype(BF16), preferred_element_type=F32)
        if act_pair:
            out = _silu(out) * jnp.dot(x, w_refs[1][...].astype(BF16), preferred_element_type=F32)
        o_ref[0:nrows, :] = out.astype(o_ref.dtype)

    @pl.when(live > half)
    def _():
        compute(tm)

    @pl.when((live > 0) & (live <= half))
    def _():
        compute(half)
        o_ref[half:tm, :] = jnp.zeros((tm - half, o_ref.shape[1]), o_ref.dtype)

    @pl.when(live == 0)
    def _():
        o_ref[...] = jnp.zeros_like(o_ref)


def moe_matmul(tile_expert, tile_rows, x, w_list, *, out_dtype, tm=MOE_ROW_TILE, tn=256,
               name="moe_matmul"):
    r, k = x.shape
    n = w_list[0].shape[2]
    assert r % tm == 0 and n % tn == 0
    kern = functools.partial(_moe_mm_kernel, act_pair=len(w_list) == 2)
    w_spec = pl.BlockSpec((None, k, tn), lambda j, i, te, tv: (te[i], 0, j))
    return pl.pallas_call(
        kern,
        grid_spec=pltpu.PrefetchScalarGridSpec(
            num_scalar_prefetch=2,
            grid=(n // tn, r // tm),
            in_specs=[pl.BlockSpec((tm, k), lambda j, i, te, tv: (i, 0))] + [w_spec] * len(w_list),
            out_specs=pl.BlockSpec((tm, tn), lambda j, i, te, tv: (i, j))),
        out_shape=jax.ShapeDtypeStruct((r, n), out_dtype),
        compiler_params=_params("parallel", "arbitrary"),
        name=name,
    )(tile_expert, tile_rows, x, *w_list)


def _combine_kernel(d0_ref, d1_ref, x_ref, route_ref, g_ref, y_hbm, o_ref, buf0_ref, buf1_ref, sems):
    rows = x_ref.shape[0]
    base = pl.program_id(0) * rows
    _row_gather(d0_ref, base, y_hbm, buf0_ref, sems.at[0], rows)
    _row_gather(d1_ref, base, y_hbm, buf1_ref, sems.at[1], rows)
    w0 = route_ref[:, 2:3]
    w1 = route_ref[:, 3:4]
    x = x_ref[...] + w0 * buf0_ref[...] + w1 * buf1_ref[...]
    y = x * lax.rsqrt(jnp.mean(x * x, axis=-1, keepdims=True) + NORM_EPS)
    o_ref[...] = y * g_ref[...]


def moe_combine_norm(dest0, dest1, x, route, y, g, tm=256):
    m, d = x.shape
    assert tm % GATHER_UNROLL == 0
    return pl.pallas_call(
        _combine_kernel,
        grid_spec=pltpu.PrefetchScalarGridSpec(
            num_scalar_prefetch=2,
            grid=(m // tm,),
            in_specs=[pl.BlockSpec((tm, d), lambda i, a, b: (i, 0)),
                      pl.BlockSpec((tm, LANES), lambda i, a, b: (i, 0)),
                      pl.BlockSpec((1, d), lambda i, a, b: (0, 0)),
                      pl.BlockSpec(memory_space=pl.ANY)],
            out_specs=pl.BlockSpec((tm, d), lambda i, a, b: (i, 0)),
            scratch_shapes=[pltpu.VMEM((tm, d), F32), pltpu.VMEM((tm, d), F32),
                            pltpu.SemaphoreType.DMA((2,))]),
        out_shape=jax.ShapeDtypeStruct((m, d), F32),
        compiler_params=_params("arbitrary"),
        name="moe_combine_norm",
    )(dest0, dest1, x, route, g.reshape(1, d), y)


def _moe_plan(ids, tm):
    n_tok = ids.shape[0]
    e_flat = ids.reshape(-1)
    onehot = (e_flat[:, None] == jnp.arange(N_EXPERTS, dtype=jnp.int32)[None, :]).astype(jnp.int32)
    csum = jnp.cumsum(onehot, axis=0)
    pos = jnp.sum(csum * onehot, axis=1) - 1
    counts = csum[-1]
    padded = ((counts + tm - 1) // tm) * tm
    gend = jnp.cumsum(padded)
    gstart = gend - padded
    dest = gstart[e_flat] + pos
    n_rows = 2 * n_tok + N_EXPERTS * tm
    row_token = jnp.zeros((n_rows,), jnp.int32).at[dest].set(
        jnp.arange(2 * n_tok, dtype=jnp.int32) // 2)
    tile_start = jnp.arange(n_rows // tm, dtype=jnp.int32) * tm
    last_start = jnp.maximum(gend[-1] - tm, 0)
    probe = jnp.minimum(tile_start, last_start)
    tile_expert = jnp.minimum(jnp.sum((probe[:, None] >= gend[None, :]).astype(jnp.int32), axis=1),
                              N_EXPERTS - 1).astype(jnp.int32)
    live_end = (gstart + counts)[tile_expert]
    tile_rows = jnp.where(tile_start < gend[-1], jnp.clip(live_end - tile_start, 0, tm), 0)
    dest2 = dest.reshape(n_tok, 2).astype(jnp.int32)
    return row_token, tile_expert, tile_rows.astype(jnp.int32), dest2[:, 0], dest2[:, 1]


def kernel(x, ev_norm1, ev_w_in, ev_lru_conv_w, ev_lru_conv_b, ev_lru_w_a, ev_lru_b_a, ev_lru_w_x, ev_lru_b_x, ev_lru_lambda, ev_w_out, ev_norm2, ev_ffn_w1, ev_ffn_w3, ev_ffn_w2, od_norm1, od_w_in, od_gla_w_a2, od_gla_b_a, od_gla_norm, od_ssd_conv_w, od_ssd_conv_b, od_ssd_dt_bias, od_ssd_a_log, od_ssd_d, od_ssd_norm, od_w_out, od_norm2, od_router, od_moe_w1, od_moe_w3, od_moe_w2, final_norm):
    bsz, t, d = x.shape
    n_tok = bsz * t
    xf = x.reshape(n_tok, d)

    lru_w = ev_lru_lambda.shape[1]
    moba_w = (ev_w_in.shape[2] - 2 * lru_w) // 3
    moba_dh = LANES
    h = rmsnorm(xf, ev_norm1[0])
    qkv = matmul([h], [ev_w_in[0]], w_col0=0, n_cols=3 * moba_w, out_dtype=BF16, name="l0_in_qkv")
    xg = matmul([h], [ev_w_in[0]], w_col0=3 * moba_w, n_cols=2 * lru_w, out_dtype=F32, name="l0_in_lru")
    att = moba_attention(qkv.reshape(bsz, t, 3 * moba_w), nh=moba_w // moba_dh, dh=moba_dh)
    rec = conv_rglru(xg.reshape(bsz, t, 2 * lru_w), ev_lru_conv_w[0], ev_lru_conv_b[0], ev_lru_w_a[0],
                     ev_lru_b_a[0], ev_lru_w_x[0], ev_lru_b_x[0], ev_lru_lambda[0])
    xf = matmul([att.reshape(n_tok, moba_w), rec.reshape(n_tok, lru_w)], [ev_w_out[0]], res=xf,
                name="l0_out")
    h = rmsnorm(xf, ev_norm2[0])
    gact = swiglu_up(h, ev_ffn_w1[0], ev_ffn_w3[0], name="l0_ffn_up")
    xf = matmul_ktiled_res(gact, ev_ffn_w2[0], xf, name="l0_ffn_down")

    gla_key = od_gla_w_a2.shape[2]
    gla_rank = od_gla_w_a2.shape[1]
    gla_dv = od_gla_norm.shape[1]
    gla_heads = gla_key // (gla_dv // 2)
    gla_val = gla_heads * gla_dv
    ssd_heads = od_ssd_a_log.shape[1]
    ssd_width = od_ssd_norm.shape[1]
    ssd_conv_dim = od_ssd_conv_w.shape[2]
    ssd_groups = 4
    ssd_state = (ssd_conv_dim - ssd_width) // (2 * ssd_groups)
    hpg = ssd_heads // ssd_groups
    w_in1_t = od_w_in[0].T
    c_lr = 2 * gla_key + 2 * gla_val
    c_z = c_lr + gla_rank
    c_dt = c_z + ssd_width + ssd_conv_dim
    zpad = lambda n: jnp.zeros((n, d), F32)
    w_small_t = jnp.concatenate(
        [w_in1_t[c_lr:c_z], zpad(LANES - gla_rank)]
        + [blk for g in range(ssd_groups)
           for blk in (w_in1_t[c_dt + g * hpg:c_dt + (g + 1) * hpg], zpad(LANES - hpg))], axis=0)

    h = rmsnorm(xf, od_norm1[0])
    p1a = matmul_wt(h, w_in1_t, row0=0, n_cols=c_lr, out_dtype=BF16, name="l1_in_gla")
    p1b = matmul_wt(h, w_in1_t, row0=c_z, n_cols=c_dt - c_z, out_dtype=F32, name="l1_in_ssd")
    sm = matmul_wt(h, w_small_t, out_dtype=F32, tn=w_small_t.shape[0], name="l1_in_small")
    w_a2p = jnp.pad(od_gla_w_a2[0], ((0, LANES - gla_rank), (0, 0)))
    o_gla = gla(p1a.reshape(bsz, t, c_lr), sm.reshape(bsz, t, -1), w_a2p, od_gla_b_a[0], od_gla_norm[0],
                nh=gla_heads, dk=gla_key // gla_heads, dv=gla_dv)
    y_ssd = ssd(p1b.reshape(bsz, t, -1), sm.reshape(bsz, t, -1), od_ssd_conv_w[0], od_ssd_conv_b[0],
                od_ssd_dt_bias[0], od_ssd_a_log[0], od_ssd_d[0], od_ssd_norm[0],
                nheads=ssd_heads, hd=ssd_width // ssd_heads, groups=ssd_groups, nstate=ssd_state)
    xf = matmul([o_gla.reshape(n_tok, gla_val), y_ssd.reshape(n_tok, ssd_width)], [od_w_out[0]], res=xf,
                name="l1_out")

    hf, route = norm_router(xf, od_norm2[0], od_router[0])
    ids = route[:, 0:2].astype(jnp.int32)
    row_token, tile_expert, tile_rows, dest0, dest1 = _moe_plan(ids, MOE_ROW_TILE)
    xs = moe_dispatch(row_token, tile_rows, hf, tile=MOE_ROW_TILE)
    gact = moe_matmul(tile_expert, tile_rows, xs, [od_moe_w1[0], od_moe_w3[0]], out_dtype=BF16,
                      tn=MOE_COL_TILE, name="moe_up")
    ye = moe_matmul(tile_expert, tile_rows, gact, [od_moe_w2[0]], out_dtype=F32,
                    tn=2 * MOE_COL_TILE, name="moe_down")
    out = moe_combine_norm(dest0, dest1, xf, route, ye, final_norm)
    return out.reshape(bsz, t, d)
```
